```python
import jax, jax.numpy as jnp
from jax import lax
import numpy as np

D_MODEL = 2048
BATCH = 4
SEQ = 4096
DEPTH = 1

N_HEADS_MLA = 8
Q_LORA_RANK = 512
KV_LORA_RANK = 512
QK_NOPE_DIM = 128
QK_ROPE_DIM = 64
QK_HEAD_DIM = QK_NOPE_DIM + QK_ROPE_DIM
V_HEAD_DIM = 128
MLA_WIDTH = N_HEADS_MLA * V_HEAD_DIM
N_HEADS_SB = 8
SB_HEAD_DIM = 128
SB_WIDTH = N_HEADS_SB * SB_HEAD_DIM
D_FF = -(-8 * D_MODEL // (3 * 256)) * 256
D_IN = Q_LORA_RANK + KV_LORA_RANK + QK_ROPE_DIM + 3 * SB_WIDTH + 2 * D_MODEL
Q_BLOCK = 128
ROPE_THETA = 10000.0
EPS = 1e-6

kernel_name = "hybrid_mla_stickbreaking_gated_block"


def _rms(x, g):
    xf = x.astype(jnp.float32)
    y = xf * lax.rsqrt(jnp.mean(xf * xf, axis=-1, keepdims=True) + EPS)
    return (y * g.astype(jnp.float32)).astype(x.dtype)


def _rope(x, pos):
    half = x.shape[-1] // 2
    freqs = ROPE_THETA ** (-jnp.arange(half, dtype=jnp.float32) / half)
    ang = pos.astype(jnp.float32)[..., None] * freqs
    cos = jnp.cos(ang)[:, :, None, :]
    sin = jnp.sin(ang)[:, :, None, :]
    xf = x.astype(jnp.float32)
    x1, x2 = xf[..., :half], xf[..., half:]
    return jnp.concatenate([x1 * cos - x2 * sin, x1 * sin + x2 * cos], axis=-1).astype(x.dtype)


def _mla_attention(q, k, v):
    S = q.shape[2]
    scale = QK_HEAD_DIM ** -0.5
    outs = []
    for i in range(S // Q_BLOCK):
        end = (i + 1) * Q_BLOCK
        qb = q[:, :, i * Q_BLOCK:end]
        s = jnp.einsum('bhqd,bhkd->bhqk', qb, k[:, :, :end]).astype(jnp.float32) * scale
        qi = i * Q_BLOCK + jnp.arange(Q_BLOCK)
        ki = jnp.arange(end)
        s = jnp.where(ki[None, :] <= qi[:, None], s, -jnp.inf)
        p = jax.nn.softmax(s, axis=-1).astype(v.dtype)
        outs.append(jnp.einsum('bhqk,bhkd->bhqd', p, v[:, :, :end]))
    return jnp.concatenate(outs, axis=2)


def _stick_breaking(q, k, v):
    S = q.shape[2]
    scale = SB_HEAD_DIM ** -0.5
    outs = []
    for i in range(S // Q_BLOCK):
        end = (i + 1) * Q_BLOCK
        qb = q[:, :, i * Q_BLOCK:end]
        z = jnp.einsum('bhqd,bhkd->bhqk', qb, k[:, :, :end]).astype(jnp.float32) * scale
        qi = i * Q_BLOCK + jnp.arange(Q_BLOCK)
        ki = jnp.arange(end)
        mask = ki[None, :] < qi[:, None]
        log_beta = jax.nn.log_sigmoid(z)
        log_one_minus = jnp.where(mask, jax.nn.log_sigmoid(-z), 0.0)
        tail = lax.cumsum(log_one_minus, axis=3, reverse=True) - log_one_minus
        a = jnp.where(mask, jnp.exp(log_beta + tail), 0.0).astype(v.dtype)
        outs.append(jnp.einsum('bhqk,bhkd->bhqd', a, v[:, :, :end]))
    return jnp.concatenate(outs, axis=2)


def _layer(x, c_act, pos, w_ada, b_ada, g_norm1, g_norm2, w_in, g_q_latent, g_kv_latent,
           w_uq, w_ukv, g_q_head, g_k_head, w_proj_mla, w_proj_sb, w_out, w_ffn_in, w_ffn_out):
    B, S, _ = x.shape
    ada = (c_act @ w_ada + b_ada)[:, None, :]
    sh1, sc1, gt1, sh2, sc2, gt2 = jnp.split(ada, 6, axis=-1)

    h = _rms(x, g_norm1) * (1 + sc1) + sh1
    proj = h @ w_in
    offs = np.cumsum([Q_LORA_RANK, KV_LORA_RANK, QK_ROPE_DIM, SB_WIDTH, SB_WIDTH, SB_WIDTH, D_MODEL])
    c_q, c_kv, k_pe, q_sb, k_sb, v_sb, gl_a, gl_b = jnp.split(proj, [int(o) for o in offs], axis=-1)

    q = (_rms(c_q, g_q_latent) @ w_uq).reshape(B, S, N_HEADS_MLA, QK_HEAD_DIM)
    kv = (_rms(c_kv, g_kv_latent) @ w_ukv).reshape(B, S, N_HEADS_MLA, QK_NOPE_DIM + V_HEAD_DIM)
    k_nope, v = kv[..., :QK_NOPE_DIM], kv[..., QK_NOPE_DIM:]
    k_pe_h = jnp.broadcast_to(k_pe[:, :, None, :], (B, S, N_HEADS_MLA, QK_ROPE_DIM))
    k = jnp.concatenate([k_nope, k_pe_h], axis=-1)
    q = _rms(q, g_q_head)
    k = _rms(k, g_k_head)
    q = jnp.concatenate([q[..., :QK_NOPE_DIM], _rope(q[..., QK_NOPE_DIM:], pos)], axis=-1)
    k = jnp.concatenate([k[..., :QK_NOPE_DIM], _rope(k[..., QK_NOPE_DIM:], pos)], axis=-1)
    y_a = _mla_attention(q.transpose(0, 2, 1, 3), k.transpose(0, 2, 1, 3), v.transpose(0, 2, 1, 3))
    y_a = y_a.transpose(0, 2, 1, 3).reshape(B, S, MLA_WIDTH)

    to_heads = lambda t: t.reshape(B, S, N_HEADS_SB, SB_HEAD_DIM).transpose(0, 2, 1, 3)
    y_b = _stick_breaking(to_heads(q_sb), to_heads(k_sb), to_heads(v_sb))
    y_b = y_b.transpose(0, 2, 1, 3).reshape(B, S, SB_WIDTH)

    merged = jax.nn.sigmoid(gl_a) * (y_a @ w_proj_mla) + jax.nn.sigmoid(gl_b) * (y_b @ w_proj_sb)
    x = x + gt1 * (merged @ w_out)

    h2 = _rms(x, g_norm2) * (1 + sc2) + sh2
    gate, up = jnp.split(h2 @ w_ffn_in, 2, axis=-1)
    x = x + gt2 * ((jax.nn.silu(gate) * up) @ w_ffn_out)
    return x


def setup_inputs(seed: int = 0) -> dict:
    key = jax.random.key(seed)
    ks = jax.random.split(key, 24)
    f32 = jnp.float32

    def nrm(k, shape, fan_in):
        return jax.random.normal(k, shape, f32) * (fan_in ** -0.5)

    def gain(k, n):
        return 1.0 + 0.02 * jax.random.normal(k, (DEPTH, n), f32)

    x = jax.random.normal(ks[0], (BATCH, SEQ, D_MODEL), f32)
    c = jax.random.normal(ks[1], (BATCH, D_MODEL), f32)
    offset = jax.random.randint(ks[2], (BATCH, 1), 0, 1024, dtype=jnp.int32)
    positions = offset + jnp.arange(SEQ, dtype=jnp.int32)[None, :]
    return {
        "x": x,
        "c": c,
        "positions": positions,
        "w_ada": nrm(ks[3], (DEPTH, D_MODEL, 6 * D_MODEL), D_MODEL),
        "b_ada": 0.02 * jax.random.normal(ks[4], (DEPTH, 6 * D_MODEL), f32),
        "g_norm1": gain(ks[5], D_MODEL),
        "g_norm2": gain(ks[6], D_MODEL),
        "w_in": nrm(ks[7], (DEPTH, D_MODEL, D_IN), D_MODEL),
        "g_q_latent": gain(ks[8], Q_LORA_RANK),
        "g_kv_latent": gain(ks[9], KV_LORA_RANK),
        "w_uq": nrm(ks[10], (DEPTH, Q_LORA_RANK, N_HEADS_MLA * QK_HEAD_DIM), Q_LORA_RANK),
        "w_ukv": nrm(ks[11], (DEPTH, KV_LORA_RANK, N_HEADS_MLA * (QK_NOPE_DIM + V_HEAD_DIM)), KV_LORA_RANK),
        "g_q_head": gain(ks[12], QK_HEAD_DIM),
        "g_k_head": gain(ks[13], QK_HEAD_DIM),
        "w_proj_mla": nrm(ks[14], (DEPTH, MLA_WIDTH, D_MODEL), MLA_WIDTH),
        "w_proj_sb": nrm(ks[15], (DEPTH, SB_WIDTH, D_MODEL), SB_WIDTH),
        "w_out": nrm(ks[16], (DEPTH, D_MODEL, D_MODEL), D_MODEL),
        "w_ffn_in": nrm(ks[17], (DEPTH, D_MODEL, 2 * D_FF), D_MODEL),
        "w_ffn_out": nrm(ks[18], (DEPTH, D_FF, D_MODEL), D_FF),
    }


def reference(x, c, positions, w_ada, b_ada, g_norm1, g_norm2, w_in, g_q_latent, g_kv_latent,
              w_uq, w_ukv, g_q_head, g_k_head, w_proj_mla, w_proj_sb, w_out, w_ffn_in, w_ffn_out):
    c_act = jax.nn.silu(c)
    for l in range(DEPTH):
        x = _layer(x, c_act, positions, w_ada[l], b_ada[l], g_norm1[l], g_norm2[l], w_in[l],
                   g_q_latent[l], g_kv_latent[l], w_uq[l], w_ukv[l], g_q_head[l], g_k_head[l],
                   w_proj_mla[l], w_proj_sb[l], w_out[l], w_ffn_in[l], w_ffn_out[l])
    return x
```

```python
import functools

import jax
import jax.numpy as jnp
import numpy as np
from jax import lax
from jax.experimental import pallas as pl
from jax.experimental.pallas import tpu as pltpu

D_MODEL = 2048
N_HEADS = 8
LATENT = 512
NOPE = 128
ROPE = 64
HALF = ROPE // 2
QK_DIM = NOPE + ROPE
QK_PAD = 256
V_DIM = 128
SB_DIM = 128
HEADS_W = N_HEADS * 128
ROPE_THETA = 10000.0
EPS = 1e-6
LANES = 128
NEG_BIG = -1e30

COL_CQ, COL_CKV, COL_QSB, COL_KSB, COL_VSB, COL_GA, COL_GB = 0, 512, 1024, 2048, 3072, 4096, 6144
PROJ_W = 8192
GATE_COL0 = COL_GA

VMEM_LIMIT = 56 * 1024 * 1024

f32 = jnp.float32
bf16 = jnp.bfloat16


def _cparams(sem):
    return pltpu.CompilerParams(dimension_semantics=sem, vmem_limit_bytes=VMEM_LIMIT)


def _dot(a, b):
    return jnp.dot(a, b, preferred_element_type=f32)


def _dot_nt(a, b):
    return lax.dot_general(a, b, (((1,), (1,)), ((), ())), preferred_element_type=f32)


def _rms_rows(xf, g_row):
    return xf * lax.rsqrt(jnp.mean(xf * xf, axis=-1, keepdims=True) + EPS) * g_row


def _ada_kernel(c_ref, w_ref, b_ref, o_ref):
    c = c_ref[...]
    c_act = (c * (1.0 / (1.0 + jnp.exp(-c)))).astype(bf16)
    o_ref[...] = _dot(c_act, w_ref[...].astype(bf16)) + b_ref[...]


def _ada(c_pad, w_ada, b_ada):
    rows, d = c_pad.shape
    n = w_ada.shape[1]
    tn = 1024
    return pl.pallas_call(
        _ada_kernel,
        out_shape=jax.ShapeDtypeStruct((rows, n), f32),
        grid=(n // tn,),
        in_specs=[pl.BlockSpec((rows, d), lambda j: (0, 0)),
                  pl.BlockSpec((d, tn), lambda j: (0, j)),
                  pl.BlockSpec((1, tn), lambda j: (0, j))],
        out_specs=pl.BlockSpec((rows, tn), lambda j: (0, j)),
        compiler_params=_cparams(("arbitrary",)),
        name="ada",
    )(c_pad, w_ada, b_ada)


def _in_proj_kernel(x_ref, ada_ref, g_ref, w_ref, wk_ref, proj_ref, kpe_ref, h_ref, *, tn):
    n = pl.program_id(1)

    @pl.when(n == 0)
    def _():
        shift = ada_ref[0, 0:1, :]
        scale = ada_ref[0, 1:2, :]
        h = _rms_rows(x_ref[...], g_ref[...]) * (1.0 + scale) + shift
        h_ref[...] = h.astype(bf16)
        kpe_ref[...] = _dot(h_ref[...], wk_ref[...])

    acc = _dot(h_ref[...], w_ref[...])

    @pl.when(n * tn < GATE_COL0)
    def _():
        proj_ref[...] = acc.astype(bf16)

    @pl.when(n * tn >= GATE_COL0)
    def _():
        proj_ref[...] = (1.0 / (1.0 + jnp.exp(-acc))).astype(bf16)


def _in_proj(x2, ada3, g1, w_main, w_kpe, seq):
    t, d = x2.shape
    tm, tn = 512, 512
    per_b = seq // tm
    return pl.pallas_call(
        functools.partial(_in_proj_kernel, tn=tn),
        out_shape=(jax.ShapeDtypeStruct((t, PROJ_W), bf16), jax.ShapeDtypeStruct((t, LANES), f32)),
        grid=(t // tm, PROJ_W // tn),
        in_specs=[pl.BlockSpec((tm, d), lambda i, n: (i, 0)),
                  pl.BlockSpec((1, 6, d), lambda i, n: (i // per_b, 0, 0)),
                  pl.BlockSpec((1, d), lambda i, n: (0, 0)),
                  pl.BlockSpec((d, tn), lambda i, n: (0, n)),
                  pl.BlockSpec((d, LANES), lambda i, n: (0, 0))],
        out_specs=(pl.BlockSpec((tm, tn), lambda i, n: (i, n)),
                   pl.BlockSpec((tm, LANES), lambda i, n: (i, 0))),
        scratch_shapes=[pltpu.VMEM((tm, d), bf16)],
        compiler_params=_cparams(("arbitrary", "arbitrary")),
        name="in_proj",
    )(x2, ada3, g1, w_main, w_kpe)


def _rope_lanes(r, cos_t, sin_lo, sin_hi):
    return r * cos_t + pltpu.roll(r, LANES - HALF, 1) * sin_lo + pltpu.roll(r, HALF, 1) * sin_hi


def _latent_up_kernel(cq_ref, ckv_ref, kpe_ref, pos_ref, freq_ref, gql_ref, gkvl_ref, wq_ref, wkv_ref,
                      gqh_ref, gkh_ref, q_ref, k_ref, v_ref, cqn_ref, ckvn_ref, cos_ref, slo_ref, shi_ref):
    h = pl.program_id(1)

    @pl.when(h == 0)
    def _():
        cqn_ref[...] = _rms_rows(cq_ref[...].astype(f32), gql_ref[...]).astype(bf16)
        ckvn_ref[...] = _rms_rows(ckv_ref[...].astype(f32), gkvl_ref[...]).astype(bf16)
        ang = pos_ref[...].astype(f32) * freq_ref[...]
        lane = lax.broadcasted_iota(jnp.int32, ang.shape, 1)
        cos_v, sin_v = jnp.cos(ang), jnp.sin(ang)
        cos_ref[...] = jnp.where(lane < ROPE, cos_v, 0.0)
        slo_ref[...] = jnp.where(lane < HALF, -sin_v, 0.0)
        shi_ref[...] = jnp.where((lane >= HALF) & (lane < ROPE), sin_v, 0.0)

    cos_t, sin_lo, sin_hi = cos_ref[...], slo_ref[...], shi_ref[...]

    q = _dot(cqn_ref[...], wq_ref[0])
    q_r = lax.rsqrt(jnp.sum(q * q, axis=-1, keepdims=True) * (1.0 / QK_DIM) + EPS)
    gq = gqh_ref[...] * (QK_DIM ** -0.5)
    q_ref[0, 0, :, 0:NOPE] = (q[:, 0:NOPE] * q_r * gq[:, 0:NOPE]).astype(bf16)
    q_ref[0, 0, :, NOPE:QK_PAD] = _rope_lanes(q[:, NOPE:QK_PAD] * q_r * gq[:, NOPE:QK_PAD],
                                              cos_t, sin_lo, sin_hi).astype(bf16)

    kv = _dot(ckvn_ref[...], wkv_ref[0])
    k_nope = kv[:, 0:NOPE]
    kpe = kpe_ref[...]
    k_ss = jnp.sum(k_nope * k_nope, axis=-1, keepdims=True) + jnp.sum(kpe * kpe, axis=-1, keepdims=True)
    k_r = lax.rsqrt(k_ss * (1.0 / QK_DIM) + EPS)
    gk = gkh_ref[...]
    k_ref[0, 0, :, 0:NOPE] = (k_nope * k_r * gk[:, 0:NOPE]).astype(bf16)
    k_ref[0, 0, :, NOPE:QK_PAD] = _rope_lanes(kpe * k_r * gk[:, NOPE:QK_PAD], cos_t, sin_lo, sin_hi).astype(bf16)
    v_ref[0, 0] = kv[:, NOPE:QK_PAD].astype(bf16)


def _latent_up(proj, kpe, pos2, freq, gql, gkvl, wq_h, wkv_h, gqh, gkh, batch, seq):
    t = proj.shape[0]
    tm = 512
    per_b = seq // tm
    head_out = lambda w: pl.BlockSpec((1, 1, tm, w), lambda i, h: (i // per_b, h, i % per_b, 0))
    row = lambda w: pl.BlockSpec((1, w), lambda i, h: (0, 0))
    return pl.pallas_call(
        _latent_up_kernel,
        out_shape=(jax.ShapeDtypeStruct((batch, N_HEADS, seq, QK_PAD), bf16),
                   jax.ShapeDtypeStruct((batch, N_HEADS, seq, QK_PAD), bf16),
                   jax.ShapeDtypeStruct((batch, N_HEADS, seq, V_DIM), bf16)),
        grid=(t // tm, N_HEADS),
        in_specs=[pl.BlockSpec((tm, LATENT), lambda i, h: (i, COL_CQ // LATENT)),
                  pl.BlockSpec((tm, LATENT), lambda i, h: (i, COL_CKV // LATENT)),
                  pl.BlockSpec((tm, LANES), lambda i, h: (i, 0)),
                  pl.BlockSpec((tm, 1), lambda i, h: (i, 0)),
                  row(LANES), row(LATENT), row(LATENT),
                  pl.BlockSpec((1, LATENT, QK_PAD), lambda i, h: (h, 0, 0)),
                  pl.BlockSpec((1, LATENT, QK_PAD), lambda i, h: (h, 0, 0)),
                  row(QK_PAD), row(QK_PAD)],
        out_specs=(head_out(QK_PAD), head_out(QK_PAD), head_out(V_DIM)),
        scratch_shapes=[pltpu.VMEM((tm, LATENT), bf16), pltpu.VMEM((tm, LATENT), bf16),
                        pltpu.VMEM((tm, LANES), f32), pltpu.VMEM((tm, LANES), f32), pltpu.VMEM((tm, LANES), f32)],
        compiler_params=_cparams(("arbitrary", "arbitrary")),
        name="latent_up",
    )(proj, proj, kpe, pos2, freq, gql, gkvl, wq_h, wkv_h, gqh, gkh)


def _mla_kernel(q_ref, k_ref, v_ref, o_ref, *, tq, tk):
    qi = pl.program_id(2)
    q = q_ref[0, 0]

    def step(k_blk, v_blk, carry, masked):
        m, l, acc = carry
        s = _dot_nt(q, k_blk)
        if masked:
            row = lax.broadcasted_iota(jnp.int32, s.shape, 0)
            col = lax.broadcasted_iota(jnp.int32, s.shape, 1)
            s = jnp.where(col <= row, s, NEG_BIG)
        m_new = jnp.maximum(m, jnp.max(s, axis=-1, keepdims=True))
        p = jnp.exp(s - m_new)
        alpha = jnp.exp(m - m_new)
        l_new = alpha * l + jnp.sum(p, axis=-1, keepdims=True)
        acc_new = alpha * acc + _dot(p.astype(bf16), v_blk)
        return m_new, l_new, acc_new

    def body(j, carry):
        off = pl.multiple_of(j * tk, tk)
        return step(k_ref[0, 0, pl.ds(off, tk), :], v_ref[0, 0, pl.ds(off, tk), :], carry, False)

    init = (jnp.full((tq, 1), NEG_BIG, f32), jnp.zeros((tq, 1), f32), jnp.zeros((tq, V_DIM), f32))
    carry = lax.fori_loop(0, qi, body, init)
    off = pl.multiple_of(qi * tk, tk)
    _, l, acc = step(k_ref[0, 0, pl.ds(off, tk), :], v_ref[0, 0, pl.ds(off, tk), :], carry, True)
    o_ref[...] = (acc / l).astype(bf16)


def _mla_attn(q, k, v):
    batch, heads, seq, _ = q.shape
    tq = tk = 256
    nq = seq // tq
    return pl.pallas_call(
        functools.partial(_mla_kernel, tq=tq, tk=tk),
        out_shape=jax.ShapeDtypeStruct((batch * seq, heads * V_DIM), bf16),
        grid=(batch, heads, nq),
        in_specs=[pl.BlockSpec((1, 1, tq, QK_PAD), lambda b, h, i: (b, h, i, 0)),
                  pl.BlockSpec((1, 1, seq, QK_PAD), lambda b, h, i: (b, h, 0, 0)),
                  pl.BlockSpec((1, 1, seq, V_DIM), lambda b, h, i: (b, h, 0, 0))],
        out_specs=pl.BlockSpec((tq, V_DIM), lambda b, h, i: (b * nq + i, h)),
        compiler_params=_cparams(("arbitrary", "arbitrary", "arbitrary")),
        name="mla_attn",
    )(q, k, v)


def _sb_kernel(q_ref, k_ref, v_ref, o_ref, *, tq, tk):
    qi = pl.program_id(2)
    q = (q_ref[...].astype(f32) * (SB_DIM ** -0.5)).astype(bf16)
    jj = lax.broadcasted_iota(jnp.int32, (tk, tk), 0)
    ss = lax.broadcasted_iota(jnp.int32, (tk, tk), 1)
    later = jnp.where(jj > ss, 1.0, 0.0).astype(bf16)
    later2 = jnp.concatenate([later, later], axis=0)

    def step(k_blk, v_blk, carry, masked):
        c, acc = carry
        z = _dot_nt(q, k_blk)
        lg = jnp.log(1.0 + jnp.exp(-jnp.abs(z)))
        log_beta = jnp.minimum(z, 0.0) - lg
        lom = -(jnp.maximum(z, 0.0) + lg)
        if masked:
            row = lax.broadcasted_iota(jnp.int32, z.shape, 0)
            col = lax.broadcasted_iota(jnp.int32, z.shape, 1)
            keep = col < row
            lom = jnp.where(keep, lom, 0.0)
        hi = lom.astype(bf16)
        lo = (lom - hi.astype(f32)).astype(bf16)
        tail = _dot(jnp.concatenate([hi, lo], axis=1), later2)
        a = jnp.exp(log_beta + tail + c)
        if masked:
            a = jnp.where(keep, a, 0.0)
        acc_new = acc + _dot(a.astype(bf16), v_blk)
        c_new = c + jnp.sum(lom, axis=-1, keepdims=True)
        return c_new, acc_new

    init = (jnp.zeros((tq, 1), f32), jnp.zeros((tq, SB_DIM), f32))
    off = pl.multiple_of(qi * tk, tk)
    carry = step(k_ref[pl.ds(off, tk), :], v_ref[pl.ds(off, tk), :], init, True)

    def body(t, carry):
        off = pl.multiple_of((qi - 1 - t) * tk, tk)
        return step(k_ref[pl.ds(off, tk), :], v_ref[pl.ds(off, tk), :], carry, False)

    _, acc = lax.fori_loop(0, qi, body, carry)
    o_ref[...] = acc.astype(bf16)


def _sb_attn(proj, batch, seq):
    tq = tk = 256
    nq = seq // tq
    q0, k0, v0 = COL_QSB // SB_DIM, COL_KSB // SB_DIM, COL_VSB // SB_DIM
    return pl.pallas_call(
        functools.partial(_sb_kernel, tq=tq, tk=tk),
        out_shape=jax.ShapeDtypeStruct((batch * seq, HEADS_W), bf16),
        grid=(batch, N_HEADS, nq),
        in_specs=[pl.BlockSpec((tq, SB_DIM), lambda b, h, i: (b * nq + i, q0 + h)),
                  pl.BlockSpec((seq, SB_DIM), lambda b, h, i: (b, k0 + h)),
                  pl.BlockSpec((seq, SB_DIM), lambda b, h, i: (b, v0 + h))],
        out_specs=pl.BlockSpec((tq, SB_DIM), lambda b, h, i: (b * nq + i, h)),
        compiler_params=_cparams(("arbitrary", "arbitrary", "arbitrary")),
        name="sb_attn",
    )(proj, proj, proj)


def _merge_out_kernel(ya_ref, yb_ref, ga_ref, gb_ref, x_ref, ada_ref, g2_ref, wa_ref, wb_ref, wo_ref,
                      x1_ref, h2_ref):
    merged = (ga_ref[...].astype(f32) * _dot(ya_ref[...], wa_ref[...])
              + gb_ref[...].astype(f32) * _dot(yb_ref[...], wb_ref[...]))
    x1 = x_ref[...] + ada_ref[0, 2:3, :] * _dot(merged.astype(bf16), wo_ref[...])
    x1_ref[...] = x1
    h2 = _rms_rows(x1, g2_ref[...]) * (1.0 + ada_ref[0, 4:5, :]) + ada_ref[0, 3:4, :]
    h2_ref[...] = h2.astype(bf16)


def _merge_out(ya, yb, proj, x2, ada3, g2, wa, wb, wo, seq):
    t, d = x2.shape
    tm = 256
    per_b = seq // tm
    const = lambda shape: pl.BlockSpec(shape, lambda i: (0, 0), pipeline_mode=pl.Buffered(1))
    return pl.pallas_call(
        _merge_out_kernel,
        out_shape=(jax.ShapeDtypeStruct((t, d), f32), jax.ShapeDtypeStruct((t, d), bf16)),
        grid=(t // tm,),
        in_specs=[pl.BlockSpec((tm, HEADS_W), lambda i: (i, 0)),
                  pl.BlockSpec((tm, HEADS_W), lambda i: (i, 0)),
                  pl.BlockSpec((tm, d), lambda i: (i, COL_GA // d)),
                  pl.BlockSpec((tm, d), lambda i: (i, COL_GB // d)),
                  pl.BlockSpec((tm, d), lambda i: (i, 0)),
                  pl.BlockSpec((1, 6, d), lambda i: (i // per_b, 0, 0)),
                  pl.BlockSpec((1, d), lambda i: (0, 0)),
                  const((HEADS_W, d)), const((HEADS_W, d)), const((d, d))],
        out_specs=(pl.BlockSpec((tm, d), lambda i: (i, 0)), pl.BlockSpec((tm, d), lambda i: (i, 0))),
        compiler_params=_cparams(("arbitrary",)),
        name="merge_out",
    )(ya, yb, proj, proj, x2, ada3, g2, wa, wb, wo)


def _ffn_kernel(h_ref, x1_ref, ada_ref, wg_ref, wu_ref, wo_ref, o_ref, acc_ref):
    f = pl.program_id(1)

    @pl.when(f == 0)
    def _():
        acc_ref[...] = jnp.zeros_like(acc_ref)

    h = h_ref[...]
    gate = _dot(h, wg_ref[...])
    up = _dot(h, wu_ref[...])
    act = gate * (1.0 / (1.0 + jnp.exp(-gate))) * up
    acc_ref[...] += _dot(act.astype(bf16), wo_ref[...])

    @pl.when(f == pl.num_programs(1) - 1)
    def _():
        o_ref[...] = x1_ref[...] + ada_ref[0, 5:6, :] * acc_ref[...]


def _ffn(h2, x1, ada3, wg, wu, wo, seq):
    t, d = x1.shape
    d_ff = wg.shape[1]
    tm, tf = 512, 512
    per_b = seq // tm
    return pl.pallas_call(
        _ffn_kernel,
        out_shape=jax.ShapeDtypeStruct((t, d), f32),
        grid=(t // tm, d_ff // tf),
        in_specs=[pl.BlockSpec((tm, d), lambda i, f: (i, 0)),
                  pl.BlockSpec((tm, d), lambda i, f: (i, 0)),
                  pl.BlockSpec((1, 6, d), lambda i, f: (i // per_b, 0, 0)),
                  pl.BlockSpec((d, tf), lambda i, f: (0, f)),
                  pl.BlockSpec((d, tf), lambda i, f: (0, f)),
                  pl.BlockSpec((tf, d), lambda i, f: (f, 0))],
        out_specs=pl.BlockSpec((tm, d), lambda i, f: (i, 0)),
        scratch_shapes=[pltpu.VMEM((tm, d), f32)],
        compiler_params=_cparams(("arbitrary", "arbitrary")),
        name="ffn",
    )(h2, x1, ada3, wg, wu, wo)


def _pad_cols(w, width):
    return jnp.pad(w, ((0, 0), (0, width - w.shape[1])))


def _prep_layer(w_in, w_uq, w_ukv, g_q_head, g_k_head, w_proj_mla, w_proj_sb, w_out, w_ffn_in, w_ffn_out):
    o_ckv, o_kpe, o_qsb = LATENT, 2 * LATENT, 2 * LATENT + ROPE
    w_main = jnp.concatenate([w_in[:, :o_kpe], w_in[:, o_qsb:]], axis=1).astype(bf16)
    w_kpe = _pad_cols(w_in[:, o_kpe:o_qsb], LANES).astype(bf16)
    wq_h = w_uq.reshape(LATENT, N_HEADS, QK_DIM).transpose(1, 0, 2)
    wq_h = jnp.pad(wq_h, ((0, 0), (0, 0), (0, QK_PAD - QK_DIM))).astype(bf16)
    wkv_h = w_ukv.reshape(LATENT, N_HEADS, NOPE + V_DIM).transpose(1, 0, 2).astype(bf16)
    gqh = _pad_cols(g_q_head[None, :], QK_PAD)
    gkh = _pad_cols(g_k_head[None, :], QK_PAD)
    d_ff = w_ffn_out.shape[0]
    return dict(w_main=w_main, w_kpe=w_kpe, wq_h=wq_h, wkv_h=wkv_h, gqh=gqh, gkh=gkh,
                wa=w_proj_mla.astype(bf16), wb=w_proj_sb.astype(bf16), wo=w_out.astype(bf16),
                wg=w_ffn_in[:, :d_ff].astype(bf16), wu=w_ffn_in[:, d_ff:].astype(bf16),
                wf=w_ffn_out.astype(bf16))


def kernel(x, c, positions, w_ada, b_ada, g_norm1, g_norm2, w_in, g_q_latent, g_kv_latent, w_uq, w_ukv,
           g_q_head, g_k_head, w_proj_mla, w_proj_sb, w_out, w_ffn_in, w_ffn_out):
    batch, seq, d = x.shape
    depth = w_ada.shape[0]
    t = batch * seq
    x2 = x.reshape(t, d)
    pos2 = positions.reshape(t, 1)
    c_pad = jnp.pad(c, ((0, 8 - batch), (0, 0)))
    lane = np.arange(LANES)
    freq = jnp.where(lane < ROPE, ROPE_THETA ** (-jnp.asarray(lane % HALF, f32) / HALF), 0.0)[None, :]

    for l in range(depth):
        p = _prep_layer(w_in[l], w_uq[l], w_ukv[l], g_q_head[l], g_k_head[l], w_proj_mla[l], w_proj_sb[l],
                        w_out[l], w_ffn_in[l], w_ffn_out[l])
        ada = _ada(c_pad, w_ada[l], b_ada[l][None, :])
        ada3 = ada[:batch].reshape(batch, 6, d)
        proj, kpe = _in_proj(x2, ada3, g_norm1[l][None, :], p["w_main"], p["w_kpe"], seq)
        q, k, v = _latent_up(proj, kpe, pos2, freq, g_q_latent[l][None, :], g_kv_latent[l][None, :],
                             p["wq_h"], p["wkv_h"], p["gqh"], p["gkh"], batch, seq)
        ya = _mla_attn(q, k, v)
        yb = _sb_attn(proj, batch, seq)
        x1, h2 = _merge_out(ya, yb, proj, x2, ada3, g_norm2[l][None, :], p["wa"], p["wb"], p["wo"], seq)
        x2 = _ffn(h2, x1, ada3, p["wg"], p["wu"], p["wf"], seq)
    return x2.reshape(batch, seq, d)
```

```python
import functools

import jax
import jax.numpy as jnp
import numpy as np
from jax import lax
from jax.experimental import pallas as pl
from jax.experimental.pallas import tpu as pltpu

D_MODEL = 2048
N_HEADS = 8
LATENT = 512
NOPE = 128
ROPE = 64
HALF = ROPE // 2
QK_DIM = NOPE + ROPE
QK_PAD = 256
V_DIM = 128
SB_DIM = 128
SUB = 128
MLA_HEADS_PER_STEP = 4
SB_HEADS_PER_STEP = 2
HEADS_W = N_HEADS * 128
ROPE_THETA = 10000.0
EPS = 1e-6
LANES = 128
NEG_BIG = -1e30
LOG2_E = 1.4426950408889634

COL_CQ, COL_CKV, COL_QSB, COL_KSB, COL_VSB, COL_GA, COL_GB = 0, 512, 1024, 2048, 3072, 4096, 6144
PROJ_W = 8192

VMEM_LIMIT = 56 * 1024 * 1024

f32 = jnp.float32
bf16 = jnp.bfloat16


def _cparams(sem):
    return pltpu.CompilerParams(dimension_semantics=sem, vmem_limit_bytes=VMEM_LIMIT)


def _dot(a, b):
    return jnp.dot(a, b, preferred_element_type=f32)


def _dot_nt(a, b):
    return lax.dot_general(a, b, (((1,), (1,)), ((), ())), preferred_element_type=f32)


def _rms_rows(xf, g_row):
    return xf * lax.rsqrt(jnp.mean(xf * xf, axis=-1, keepdims=True) + EPS) * g_row


def _ada_kernel(c_ref, w_ref, b_ref, o_ref):
    c = c_ref[...]
    c_act = (c * (1.0 / (1.0 + jnp.exp(-c)))).astype(bf16)
    o_ref[...] = _dot(c_act, w_ref[...].astype(bf16)) + b_ref[...]


def _ada(c_pad, w_ada, b_ada):
    rows, d = c_pad.shape
    n = w_ada.shape[1]
    tn = 1024
    return pl.pallas_call(
        _ada_kernel,
        out_shape=jax.ShapeDtypeStruct((rows, n), f32),
        grid=(n // tn,),
        in_specs=[pl.BlockSpec((rows, d), lambda j: (0, 0)),
                  pl.BlockSpec((d, tn), lambda j: (0, j)),
                  pl.BlockSpec((1, tn), lambda j: (0, j))],
        out_specs=pl.BlockSpec((rows, tn), lambda j: (0, j)),
        compiler_params=_cparams(("arbitrary",)),
        name="ada",
    )(c_pad, w_ada, b_ada)


def _in_proj_kernel(x_ref, ada_ref, g_ref, w_ref, wk_ref, proj_ref, kpe_ref, h_ref):
    n = pl.program_id(1)

    @pl.when(n == 0)
    def _():
        shift = ada_ref[0, 0:1, :]
        scale = ada_ref[0, 1:2, :]
        h = _rms_rows(x_ref[...], g_ref[...]) * (1.0 + scale) + shift
        h_ref[...] = h.astype(bf16)
        kpe_ref[...] = _dot(h_ref[...], wk_ref[...])

    proj_ref[...] = _dot(h_ref[...], w_ref[...]).astype(bf16)


def _in_proj(x2, ada3, g1, w_main, w_kpe, seq):
    t, d = x2.shape
    tm, tn = 512, 1024
    per_b = seq // tm
    return pl.pallas_call(
        _in_proj_kernel,
        out_shape=(jax.ShapeDtypeStruct((t, PROJ_W), bf16), jax.ShapeDtypeStruct((t, LANES), f32)),
        grid=(t // tm, PROJ_W // tn),
        in_specs=[pl.BlockSpec((tm, d), lambda i, n: (i, 0)),
                  pl.BlockSpec((1, 6, d), lambda i, n: (i // per_b, 0, 0)),
                  pl.BlockSpec((1, d), lambda i, n: (0, 0)),
                  pl.BlockSpec((d, tn), lambda i, n: (0, n)),
                  pl.BlockSpec((d, LANES), lambda i, n: (0, 0))],
        out_specs=(pl.BlockSpec((tm, tn), lambda i, n: (i, n)),
                   pl.BlockSpec((tm, LANES), lambda i, n: (i, 0))),
        scratch_shapes=[pltpu.VMEM((tm, d), bf16)],
        compiler_params=_cparams(("arbitrary", "arbitrary")),
        name="in_proj",
    )(x2, ada3, g1, w_main, w_kpe)


def _rope_lanes(r, cos_t, sin_lo, sin_hi):
    return r * cos_t + pltpu.roll(r, LANES - HALF, 1) * sin_lo + pltpu.roll(r, HALF, 1) * sin_hi


def _latent_up_kernel(cq_ref, ckv_ref, kpe_ref, pos_ref, freq_ref, gql_ref, gkvl_ref, wq_ref, wkv_ref,
                      gqh_ref, gkh_ref, q_ref, k_ref, vt_ref, cqn_ref, ckvn_ref, cos_ref, slo_ref, shi_ref):
    h = pl.program_id(1)

    @pl.when(h == 0)
    def _():
        cqn_ref[...] = _rms_rows(cq_ref[...].astype(f32), gql_ref[...]).astype(bf16)
        ckvn_ref[...] = _rms_rows(ckv_ref[...].astype(f32), gkvl_ref[...]).astype(bf16)
        ang = pos_ref[...].astype(f32) * freq_ref[...]
        lane = lax.broadcasted_iota(jnp.int32, ang.shape, 1)
        cos_v, sin_v = jnp.cos(ang), jnp.sin(ang)
        cos_ref[...] = jnp.where(lane < ROPE, cos_v, 0.0)
        slo_ref[...] = jnp.where(lane < HALF, -sin_v, 0.0)
        shi_ref[...] = jnp.where((lane >= HALF) & (lane < ROPE), sin_v, 0.0)

    cos_t, sin_lo, sin_hi = cos_ref[...], slo_ref[...], shi_ref[...]

    q = _dot(cqn_ref[...], wq_ref[0])
    q_r = lax.rsqrt(jnp.sum(q * q, axis=-1, keepdims=True) * (1.0 / QK_DIM) + EPS)
    gq = gqh_ref[...] * (LOG2_E * QK_DIM ** -0.5)
    q_ref[0, 0, :, 0:NOPE] = (q[:, 0:NOPE] * q_r * gq[:, 0:NOPE]).astype(bf16)
    q_ref[0, 0, :, NOPE:QK_PAD] = _rope_lanes(q[:, NOPE:QK_PAD] * q_r * gq[:, NOPE:QK_PAD],
                                              cos_t, sin_lo, sin_hi).astype(bf16)

    kv = _dot(ckvn_ref[...], wkv_ref[0])
    k_nope = kv[:, 0:NOPE]
    kpe = kpe_ref[...]
    k_ss = jnp.sum(k_nope * k_nope, axis=-1, keepdims=True) + jnp.sum(kpe * kpe, axis=-1, keepdims=True)
    k_r = lax.rsqrt(k_ss * (1.0 / QK_DIM) + EPS)
    gk = gkh_ref[...]
    k_ref[0, 0, :, 0:NOPE] = (k_nope * k_r * gk[:, 0:NOPE]).astype(bf16)
    k_ref[0, 0, :, NOPE:QK_PAD] = _rope_lanes(kpe * k_r * gk[:, NOPE:QK_PAD], cos_t, sin_lo, sin_hi).astype(bf16)
    vt_ref[0, 0] = kv[:, NOPE:QK_PAD].T.astype(bf16)


def _latent_up(proj, kpe, pos2, freq, gql, gkvl, wq_h, wkv_h, gqh, gkh, batch, seq):
    t = proj.shape[0]
    tm = 512
    per_b = seq // tm
    head_out = lambda w: pl.BlockSpec((1, 1, tm, w), lambda i, h: (i // per_b, h, i % per_b, 0))
    row = lambda w: pl.BlockSpec((1, w), lambda i, h: (0, 0))
    return pl.pallas_call(
        _latent_up_kernel,
        out_shape=(jax.ShapeDtypeStruct((batch, N_HEADS, seq, QK_PAD), bf16),
                   jax.ShapeDtypeStruct((batch, N_HEADS, seq, QK_PAD), bf16),
                   jax.ShapeDtypeStruct((batch, N_HEADS, V_DIM, seq), bf16)),
        grid=(t // tm, N_HEADS),
        in_specs=[pl.BlockSpec((tm, LATENT), lambda i, h: (i, COL_CQ // LATENT)),
                  pl.BlockSpec((tm, LATENT), lambda i, h: (i, COL_CKV // LATENT)),
                  pl.BlockSpec((tm, LANES), lambda i, h: (i, 0)),
                  pl.BlockSpec((tm, 1), lambda i, h: (i, 0)),
                  row(LANES), row(LATENT), row(LATENT),
                  pl.BlockSpec((1, LATENT, QK_PAD), lambda i, h: (h, 0, 0)),
                  pl.BlockSpec((1, LATENT, QK_PAD), lambda i, h: (h, 0, 0)),
                  row(QK_PAD), row(QK_PAD)],
        out_specs=(head_out(QK_PAD), head_out(QK_PAD),
                   pl.BlockSpec((1, 1, V_DIM, tm), lambda i, h: (i // per_b, h, 0, i % per_b))),
        scratch_shapes=[pltpu.VMEM((tm, LATENT), bf16), pltpu.VMEM((tm, LATENT), bf16),
                        pltpu.VMEM((tm, LANES), f32), pltpu.VMEM((tm, LANES), f32), pltpu.VMEM((tm, LANES), f32)],
        compiler_params=_cparams(("arbitrary", "arbitrary")),
        name="latent_up",
    )(proj, proj, kpe, pos2, freq, gql, gkvl, wq_h, wkv_h, gqh, gkh)


def _mla_kernel(q_ref, k_ref, vt_ref, o_ref, acc_ref, *, tq, tk, g):
    qi = pl.program_id(2)

    def step(off, carry, masked, key0=0):
        out = []
        scores = [_dot_nt(k_ref[0, hh, pl.ds(off, tk), :], q_ref[0, hh]) for hh in range(g)]
        for hh in range(g):
            m, l = carry[hh]
            s = scores[hh]
            if masked:
                key = lax.broadcasted_iota(jnp.int32, s.shape, 0) + key0
                qry = lax.broadcasted_iota(jnp.int32, s.shape, 1)
                s = jnp.where(key <= qry, s, NEG_BIG)
            m_new = jnp.maximum(m, jnp.max(s, axis=0, keepdims=True))
            p = jnp.exp2(s - m_new)
            alpha = jnp.exp2(m - m_new)
            l_new = alpha * l + jnp.sum(p, axis=0, keepdims=True)
            acc_ref[hh] = alpha * acc_ref[hh] + _dot(vt_ref[0, hh, :, pl.ds(off, tk)], p.astype(bf16))
            out.append((m_new, l_new))
        return tuple(out)

    acc_ref[...] = jnp.zeros_like(acc_ref)
    init = tuple((jnp.full((1, tq), NEG_BIG, f32), jnp.zeros((1, tq), f32)) for _ in range(g))
    n_full = qi * (tq // tk)
    carry = lax.fori_loop(0, n_full, lambda j, c: step(pl.multiple_of(j * tk, tk), c, False), init)
    for d in range(tq // tk):
        carry = step(pl.multiple_of(qi * tq + d * tk, tk), carry, True, d * tk)
    for hh in range(g):
        o_ref[:, hh * V_DIM:(hh + 1) * V_DIM] = (acc_ref[hh] / carry[hh][1]).T.astype(bf16)


def _mla_attn(q, k, vt):
    batch, heads, seq, _ = q.shape
    tq, tk = 512, 512
    g = MLA_HEADS_PER_STEP
    nq = seq // tq
    return pl.pallas_call(
        functools.partial(_mla_kernel, tq=tq, tk=tk, g=g),
        out_shape=jax.ShapeDtypeStruct((batch * seq, heads * V_DIM), bf16),
        grid=(batch, heads // g, nq),
        in_specs=[pl.BlockSpec((1, g, tq, QK_PAD), lambda b, h, i: (b, h, i, 0)),
                  pl.BlockSpec((1, g, seq, QK_PAD), lambda b, h, i: (b, h, 0, 0)),
                  pl.BlockSpec((1, g, V_DIM, seq), lambda b, h, i: (b, h, 0, 0))],
        out_specs=pl.BlockSpec((tq, g * V_DIM), lambda b, h, i: (b * nq + i, h)),
        scratch_shapes=[pltpu.VMEM((g, V_DIM, tq), f32)],
        compiler_params=_cparams(("arbitrary", "arbitrary", "arbitrary")),
        name="mla_attn",
    )(q, k, vt)


def _sb_kernel(q_ref, k_ref, v_ref, o_ref, acc_ref, *, tq, tk, g):
    qi = pl.program_id(2)
    n_sub = tk // SUB
    ss = lax.broadcasted_iota(jnp.int32, (SUB, SUB), 0)
    jj = lax.broadcasted_iota(jnp.int32, (SUB, SUB), 1)
    later = jnp.where(jj > ss, -1.0, 0.0).astype(bf16)
    later2 = jnp.concatenate([later, later], axis=1)
    sign_bit = jnp.uint32(0x80000000)

    def head_step(hh, z, off, c, masked):
        cols = slice(hh * SB_DIM, (hh + 1) * SB_DIM)
        neg_abs = lax.bitcast_convert_type(lax.bitcast_convert_type(z, jnp.uint32) | sign_bit, f32)
        nlom = jnp.maximum(z, 0.0) + jnp.log(1.0 + jnp.exp2(neg_abs)) * LOG2_E
        log_beta = z - nlom
        if masked:
            key = lax.broadcasted_iota(jnp.int32, z.shape, 0)
            qry = lax.broadcasted_iota(jnp.int32, z.shape, 1)
            keep = key < qry
            nlom = jnp.where(keep, nlom, 0.0)
        a_parts = [None] * n_sub
        for sb in reversed(range(n_sub)):
            rows = slice(sb * SUB, (sb + 1) * SUB)
            nlom_sb = nlom[rows, :]
            hi = nlom_sb.astype(bf16)
            lo = (nlom_sb - hi.astype(f32)).astype(bf16)
            tail = _dot(later2, jnp.concatenate([hi, lo], axis=0))
            a_sb = jnp.exp2(log_beta[rows, :] + tail + c)
            if masked:
                a_sb = jnp.where(keep[rows, :], a_sb, 0.0)
            a_parts[sb] = a_sb.astype(bf16)
            c = c + (tail[0:1, :] - nlom_sb[0:1, :])
        a = jnp.concatenate(a_parts, axis=0)
        acc_ref[hh] += lax.dot_general(v_ref[pl.ds(off, tk), cols], a, (((0,), (0,)), ((), ())),
                                       preferred_element_type=f32)
        return c

    def step(off, carry, masked):
        zs = []
        for hh in range(g):
            cols = slice(hh * SB_DIM, (hh + 1) * SB_DIM)
            q = (q_ref[:, cols].astype(f32) * (LOG2_E * SB_DIM ** -0.5)).astype(bf16)
            zs.append(_dot_nt(k_ref[pl.ds(off, tk), cols], q))
        return tuple(head_step(hh, zs[hh], off, carry[hh], masked) for hh in range(g))

    acc_ref[...] = jnp.zeros_like(acc_ref)
    carry = step(pl.multiple_of(qi * tk, tk), tuple(jnp.zeros((1, tq), f32) for _ in range(g)), True)
    lax.fori_loop(0, qi, lambda t, c: step(pl.multiple_of((qi - 1 - t) * tk, tk), c, False), carry)
    for hh in range(g):
        o_ref[:, hh * SB_DIM:(hh + 1) * SB_DIM] = acc_ref[hh].T.astype(bf16)


def _sb_attn(proj, batch, seq):
    tq = tk = 512
    g = SB_HEADS_PER_STEP
    w = g * SB_DIM
    nq = seq // tq
    q0, k0, v0 = COL_QSB // w, COL_KSB // w, COL_VSB // w
    return pl.pallas_call(
        functools.partial(_sb_kernel, tq=tq, tk=tk, g=g),
        out_shape=jax.ShapeDtypeStruct((batch * seq, HEADS_W), bf16),
        grid=(batch, N_HEADS // g, nq),
        in_specs=[pl.BlockSpec((tq, w), lambda b, h, i: (b * nq + i, q0 + h)),
                  pl.BlockSpec((seq, w), lambda b, h, i: (b, k0 + h)),
                  pl.BlockSpec((seq, w), lambda b, h, i: (b, v0 + h))],
        out_specs=pl.BlockSpec((tq, w), lambda b, h, i: (b * nq + i, h)),
        scratch_shapes=[pltpu.VMEM((g, SB_DIM, tq), f32)],
        compiler_params=_cparams(("arbitrary", "arbitrary", "arbitrary")),
        name="sb_attn",
    )(proj, proj, proj)


def _merge_out_kernel(ya_ref, yb_ref, ga_ref, gb_ref, x_ref, ada_ref, g2_ref, wa_ref, wb_ref, wo_ref,
                      x1_ref, h2_ref):
    sig = lambda ref: 1.0 / (1.0 + jnp.exp(-ref[...].astype(f32)))
    merged = sig(ga_ref) * _dot(ya_ref[...], wa_ref[...]) + sig(gb_ref) * _dot(yb_ref[...], wb_ref[...])
    x1 = x_ref[...] + ada_ref[0, 2:3, :] * _dot(merged.astype(bf16), wo_ref[...])
    x1_ref[...] = x1
    h2 = _rms_rows(x1, g2_ref[...]) * (1.0 + ada_ref[0, 4:5, :]) + ada_ref[0, 3:4, :]
    h2_ref[...] = h2.astype(bf16)


def _merge_out(ya, yb, proj, x2, ada3, g2, wa, wb, wo, seq):
    t, d = x2.shape
    tm = 256
    per_b = seq // tm
    const = lambda shape: pl.BlockSpec(shape, lambda i: (0, 0), pipeline_mode=pl.Buffered(1))
    return pl.pallas_call(
        _merge_out_kernel,
        out_shape=(jax.ShapeDtypeStruct((t, d), f32), jax.ShapeDtypeStruct((t, d), bf16)),
        grid=(t // tm,),
        in_specs=[pl.BlockSpec((tm, HEADS_W), lambda i: (i, 0)),
                  pl.BlockSpec((tm, HEADS_W), lambda i: (i, 0)),
                  pl.BlockSpec((tm, d), lambda i: (i, COL_GA // d)),
                  pl.BlockSpec((tm, d), lambda i: (i, COL_GB // d)),
                  pl.BlockSpec((tm, d), lambda i: (i, 0)),
                  pl.BlockSpec((1, 6, d), lambda i: (i // per_b, 0, 0)),
                  pl.BlockSpec((1, d), lambda i: (0, 0)),
                  const((HEADS_W, d)), const((HEADS_W, d)), const((d, d))],
        out_specs=(pl.BlockSpec((tm, d), lambda i: (i, 0)), pl.BlockSpec((tm, d), lambda i: (i, 0))),
        compiler_params=_cparams(("arbitrary",)),
        name="merge_out",
    )(ya, yb, proj, proj, x2, ada3, g2, wa, wb, wo)


def _ffn_kernel(h_ref, x1_ref, ada_ref, wg_ref, wu_ref, wo_ref, o_ref, acc_ref):
    f = pl.program_id(1)

    @pl.when(f == 0)
    def _():
        acc_ref[...] = jnp.zeros_like(acc_ref)

    h = h_ref[...]
    gate = _dot(h, wg_ref[...])
    up = _dot(h, wu_ref[...])
    act = gate * (1.0 / (1.0 + jnp.exp(-gate))) * up
    acc_ref[...] += _dot(act.astype(bf16), wo_ref[...])

    @pl.when(f == pl.num_programs(1) - 1)
    def _():
        o_ref[...] = x1_ref[...] + ada_ref[0, 5:6, :] * acc_ref[...]


def _ffn(h2, x1, ada3, wg, wu, wo, seq):
    t, d = x1.shape
    d_ff = wg.shape[1]
    tm, tf = 512, 512
    per_b = seq // tm
    return pl.pallas_call(
        _ffn_kernel,
        out_shape=jax.ShapeDtypeStruct((t, d), f32),
        grid=(t // tm, d_ff // tf),
        in_specs=[pl.BlockSpec((tm, d), lambda i, f: (i, 0)),
                  pl.BlockSpec((tm, d), lambda i, f: (i, 0)),
                  pl.BlockSpec((1, 6, d), lambda i, f: (i // per_b, 0, 0)),
                  pl.BlockSpec((d, tf), lambda i, f: (0, f)),
                  pl.BlockSpec((d, tf), lambda i, f: (0, f)),
                  pl.BlockSpec((tf, d), lambda i, f: (f, 0))],
        out_specs=pl.BlockSpec((tm, d), lambda i, f: (i, 0)),
        scratch_shapes=[pltpu.VMEM((tm, d), f32)],
        compiler_params=_cparams(("arbitrary", "arbitrary")),
        name="ffn",
    )(h2, x1, ada3, wg, wu, wo)


def _pad_cols(w, width):
    return jnp.pad(w, ((0, 0), (0, width - w.shape[1])))


def _prep_layer(w_in, w_uq, w_ukv, g_q_head, g_k_head, w_proj_mla, w_proj_sb, w_out, w_ffn_in, w_ffn_out):
    o_ckv, o_kpe, o_qsb = LATENT, 2 * LATENT, 2 * LATENT + ROPE
    w_main = jnp.concatenate([w_in[:, :o_kpe], w_in[:, o_qsb:]], axis=1).astype(bf16)
    w_kpe = _pad_cols(w_in[:, o_kpe:o_qsb], LANES).astype(bf16)
    wq_h = w_uq.reshape(LATENT, N_HEADS, QK_DIM).transpose(1, 0, 2)
    wq_h = jnp.pad(wq_h, ((0, 0), (0, 0), (0, QK_PAD - QK_DIM))).astype(bf16)
    wkv_h = w_ukv.reshape(LATENT, N_HEADS, NOPE + V_DIM).transpose(1, 0, 2).astype(bf16)
    gqh = _pad_cols(g_q_head[None, :], QK_PAD)
    gkh = _pad_cols(g_k_head[None, :], QK_PAD)
    d_ff = w_ffn_out.shape[0]
    return dict(w_main=w_main, w_kpe=w_kpe, wq_h=wq_h, wkv_h=wkv_h, gqh=gqh, gkh=gkh,
                wa=w_proj_mla.astype(bf16), wb=w_proj_sb.astype(bf16), wo=w_out.astype(bf16),
                wg=w_ffn_in[:, :d_ff].astype(bf16), wu=w_ffn_in[:, d_ff:].astype(bf16),
                wf=w_ffn_out.astype(bf16))


def kernel(x, c, positions, w_ada, b_ada, g_norm1, g_norm2, w_in, g_q_latent, g_kv_latent, w_uq, w_ukv,
           g_q_head, g_k_head, w_proj_mla, w_proj_sb, w_out, w_ffn_in, w_ffn_out):
    batch, seq, d = x.shape
    depth = w_ada.shape[0]
    t = batch * seq
    x2 = x.reshape(t, d)
    pos2 = positions.reshape(t, 1)
    c_pad = jnp.pad(c, ((0, 8 - batch), (0, 0)))
    lane = np.arange(LANES)
    freq = jnp.where(lane < ROPE, ROPE_THETA ** (-jnp.asarray(lane % HALF, f32) / HALF), 0.0)[None, :]

    for l in range(depth):
        p = _prep_layer(w_in[l], w_uq[l], w_ukv[l], g_q_head[l], g_k_head[l], w_proj_mla[l], w_proj_sb[l],
                        w_out[l], w_ffn_in[l], w_ffn_out[l])
        ada = _ada(c_pad, w_ada[l], b_ada[l][None, :])
        ada3 = ada[:batch].reshape(batch, 6, d)
        proj, kpe = _in_proj(x2, ada3, g_norm1[l][None, :], p["w_main"], p["w_kpe"], seq)
        q, k, vt = _latent_up(proj, kpe, pos2, freq, g_q_latent[l][None, :], g_kv_latent[l][None, :],
                              p["wq_h"], p["wkv_h"], p["gqh"], p["gkh"], batch, seq)
        ya = _mla_attn(q, k, vt)
        yb = _sb_attn(proj, batch, seq)
        x1, h2 = _merge_out(ya, yb, proj, x2, ada3, g_norm2[l][None, :], p["wa"], p["wb"], p["wo"], seq)
        x2 = _ffn(h2, x1, ada3, p["wg"], p["wu"], p["wf"], seq)
    return x2.reshape(batch, seq, d)
```

```python
import functools

import jax
import jax.numpy as jnp
import numpy as np
from jax import lax
from jax.experimental import pallas as pl
from jax.experimental.pallas import tpu as pltpu

D_MODEL = 2048
N_HEADS = 8
LATENT = 512
NOPE = 128
ROPE = 64
HALF = ROPE // 2
QK_DIM = NOPE + ROPE
QK_PAD = 256
V_DIM = 128
SB_DIM = 128
SUB = 128
MLA_HEADS_PER_STEP = 4
SB_HEADS_PER_STEP = 4
HEADS_W = N_HEADS * 128
ROPE_THETA = 10000.0
EPS = 1e-6
LANES = 128
NEG_BIG = -1e30
LOG2_E = 1.4426950408889634
SB_DEAD_LOG2 = -150.0

COL_CQ, COL_CKV, COL_QSB, COL_KSB, COL_VSB, COL_GA, COL_GB = 0, 512, 1024, 2048, 3072, 4096, 6144
PROJ_W = 8192

VMEM_LIMIT = 56 * 1024 * 1024

f32 = jnp.float32
bf16 = jnp.bfloat16


def _cparams(sem):
    return pltpu.CompilerParams(dimension_semantics=sem, vmem_limit_bytes=VMEM_LIMIT)


def _dot(a, b):
    return jnp.dot(a, b, preferred_element_type=f32)


def _dot_nt(a, b):
    return lax.dot_general(a, b, (((1,), (1,)), ((), ())), preferred_element_type=f32)


def _rms_rows(xf, g_row):
    return xf * lax.rsqrt(jnp.mean(xf * xf, axis=-1, keepdims=True) + EPS) * g_row


def _ada_kernel(c_ref, w_ref, b_ref, o_ref):
    c = c_ref[...]
    c_act = (c * (1.0 / (1.0 + jnp.exp(-c)))).astype(bf16)
    o_ref[...] = _dot(c_act, w_ref[...].astype(bf16)) + b_ref[...]


def _ada(c_pad, w_ada, b_ada):
    rows, d = c_pad.shape
    n = w_ada.shape[1]
    tn = 1024
    return pl.pallas_call(
        _ada_kernel,
        out_shape=jax.ShapeDtypeStruct((rows, n), f32),
        grid=(n // tn,),
        in_specs=[pl.BlockSpec((rows, d), lambda j: (0, 0)),
                  pl.BlockSpec((d, tn), lambda j: (0, j)),
                  pl.BlockSpec((1, tn), lambda j: (0, j))],
        out_specs=pl.BlockSpec((rows, tn), lambda j: (0, j)),
        compiler_params=_cparams(("arbitrary",)),
        name="ada",
    )(c_pad, w_ada, b_ada)


def _in_proj_kernel(x_ref, ada_ref, g_ref, wl_ref, w_ref, wk_ref, proj_ref, kpe_ref, h_ref):
    n = pl.program_id(1)

    @pl.when(n == 0)
    def _():
        shift = ada_ref[0, 0:1, :]
        scale = ada_ref[0, 1:2, :]
        h = _rms_rows(x_ref[...], g_ref[...]) * (1.0 + scale) + shift
        h_ref[...] = h.astype(bf16)
        kpe_ref[...] = _dot(h_ref[...], wk_ref[...])
        proj_ref[...] = _dot(h_ref[...], wl_ref[...]).astype(bf16)

    @pl.when(n > 0)
    def _():
        proj_ref[...] = _dot(h_ref[...], w_ref[...]).astype(bf16)


def _in_proj(x2, ada3, g1, w_lat, w_rest, w_kpe, seq):
    t, d = x2.shape
    tm = 512
    tn = w_lat.shape[1]
    per_b = seq // tm
    return pl.pallas_call(
        _in_proj_kernel,
        out_shape=(jax.ShapeDtypeStruct((t, PROJ_W), bf16), jax.ShapeDtypeStruct((t, 2 * LANES), f32)),
        grid=(t // tm, PROJ_W // tn),
        in_specs=[pl.BlockSpec((tm, d), lambda i, n: (i, 0)),
                  pl.BlockSpec((1, 6, d), lambda i, n: (i // per_b, 0, 0)),
                  pl.BlockSpec((1, d), lambda i, n: (0, 0)),
                  pl.BlockSpec((d, tn), lambda i, n: (0, 0)),
                  pl.BlockSpec((d, tn), lambda i, n: (0, jnp.maximum(n - 1, 0))),
                  pl.BlockSpec((d, 2 * LANES), lambda i, n: (0, 0))],
        out_specs=(pl.BlockSpec((tm, tn), lambda i, n: (i, n)),
                   pl.BlockSpec((tm, 2 * LANES), lambda i, n: (i, 0))),
        scratch_shapes=[pltpu.VMEM((tm, d), bf16)],
        compiler_params=_cparams(("arbitrary", "arbitrary")),
        name="in_proj",
    )(x2, ada3, g1, w_lat, w_rest, w_kpe)


def _row_sumsq(x):
    return _dot((x * x).astype(bf16), jnp.ones((x.shape[1], LANES), bf16))


def _latent_up_kernel(cq_ref, ckv_ref, kpe_ref, pos_ref, freq_ref, gql_ref, gkvl_ref, wq_ref, wkv_ref,
                      gqh_ref, gkh_ref, q_ref, k_ref, vt_ref, cqn_ref, ckvn_ref, cos_ref, sin_ref, kss_ref):
    h = pl.program_id(1)

    @pl.when(h == 0)
    def _():
        cqn_ref[...] = _rms_rows(cq_ref[...].astype(f32), gql_ref[...]).astype(bf16)
        ckvn_ref[...] = _rms_rows(ckv_ref[...].astype(f32), gkvl_ref[...]).astype(bf16)
        ang = pos_ref[...].astype(f32) * freq_ref[...]
        cos_ref[...] = jnp.cos(ang)
        sin_ref[...] = jnp.sin(ang)
        kss_ref[...] = _row_sumsq(kpe_ref[:, 0:LANES])

    cos_t, sin_t = cos_ref[...], sin_ref[...]

    q = _dot(cqn_ref[...], wq_ref[0])
    q_r = lax.rsqrt(_row_sumsq(q[:, 0:QK_PAD]) * (1.0 / QK_DIM) + EPS)
    gq = gqh_ref[...] * (LOG2_E * QK_DIM ** -0.5)
    q_ref[0, 0, :, 0:NOPE] = (q[:, 0:NOPE] * q_r * gq[:, 0:NOPE]).astype(bf16)
    q_ref[0, 0, :, NOPE:QK_PAD] = (q[:, NOPE:QK_PAD] * q_r * gq[:, NOPE:QK_PAD] * cos_t
                                   + q[:, QK_PAD:] * q_r * gq[:, QK_PAD:] * sin_t).astype(bf16)

    kv = _dot(ckvn_ref[...], wkv_ref[0])
    k_nope = kv[:, 0:NOPE]
    k_r = lax.rsqrt((_row_sumsq(k_nope) + kss_ref[...]) * (1.0 / QK_DIM) + EPS)
    gk = gkh_ref[...]
    k_ref[0, 0, :, 0:NOPE] = (k_nope * k_r * gk[:, 0:NOPE]).astype(bf16)
    k_ref[0, 0, :, NOPE:QK_PAD] = (kpe_ref[:, 0:LANES] * k_r * gk[:, NOPE:QK_PAD] * cos_t
                                   + kpe_ref[:, LANES:] * k_r * gk[:, QK_PAD:] * sin_t).astype(bf16)
    vt_ref[0, 0] = kv[:, NOPE:QK_PAD].T.astype(bf16)


def _latent_up(proj, kpe, pos2, freq, gql, gkvl, wq_h, wkv_h, gqh, gkh, batch, seq):
    t = proj.shape[0]
    tm = 512
    per_b = seq // tm
    q_w = wq_h.shape[2]
    head_out = lambda w: pl.BlockSpec((1, 1, tm, w), lambda i, h: (i // per_b, h, i % per_b, 0))
    row = lambda w: pl.BlockSpec((1, w), lambda i, h: (0, 0))
    return pl.pallas_call(
        _latent_up_kernel,
        out_shape=(jax.ShapeDtypeStruct((batch, N_HEADS, seq, QK_PAD), bf16),
                   jax.ShapeDtypeStruct((batch, N_HEADS, seq, QK_PAD), bf16),
                   jax.ShapeDtypeStruct((batch, N_HEADS, V_DIM, seq), bf16)),
        grid=(t // tm, N_HEADS),
        in_specs=[pl.BlockSpec((tm, LATENT), lambda i, h: (i, COL_CQ // LATENT)),
                  pl.BlockSpec((tm, LATENT), lambda i, h: (i, COL_CKV // LATENT)),
                  pl.BlockSpec((tm, 2 * LANES), lambda i, h: (i, 0)),
                  pl.BlockSpec((tm, 1), lambda i, h: (i, 0)),
                  row(LANES), row(LATENT), row(LATENT),
                  pl.BlockSpec((1, LATENT, q_w), lambda i, h: (h, 0, 0)),
                  pl.BlockSpec((1, LATENT, QK_PAD), lambda i, h: (h, 0, 0)),
                  row(q_w), row(q_w)],
        out_specs=(head_out(QK_PAD), head_out(QK_PAD),
                   pl.BlockSpec((1, 1, V_DIM, tm), lambda i, h: (i // per_b, h, 0, i % per_b))),
        scratch_shapes=[pltpu.VMEM((tm, LATENT), bf16), pltpu.VMEM((tm, LATENT), bf16),
                        pltpu.VMEM((tm, LANES), f32), pltpu.VMEM((tm, LANES), f32), pltpu.VMEM((tm, LANES), f32)],
        compiler_params=_cparams(("arbitrary", "arbitrary")),
        name="latent_up",
    )(proj, proj, kpe, pos2, freq, gql, gkvl, wq_h, wkv_h, gqh, gkh)


def _mla_kernel(q_ref, k_ref, vt_ref, o_ref, acc_ref, *, tq, tk, g):
    qi = pl.program_id(2)

    def step(off, carry, masked):
        out = []
        scores = [_dot_nt(k_ref[0, hh, pl.ds(off, tk), :], q_ref[0, hh]) for hh in range(g)]
        for hh in range(g):
            m, l = carry[hh]
            s = scores[hh]
            if masked:
                key = lax.broadcasted_iota(jnp.int32, s.shape, 0)
                qry = lax.broadcasted_iota(jnp.int32, s.shape, 1)
                s = jnp.where(key <= qry, s, NEG_BIG)
            m_new = jnp.maximum(m, jnp.max(s, axis=0, keepdims=True))
            p = jnp.exp2(s - m_new)
            alpha = jnp.exp2(m - m_new)
            l_new = alpha * l + jnp.sum(p, axis=0, keepdims=True)
            acc_ref[hh] = alpha * acc_ref[hh] + _dot(vt_ref[0, hh, :, pl.ds(off, tk)], p.astype(bf16))
            out.append((m_new, l_new))
        return tuple(out)

    acc_ref[...] = jnp.zeros_like(acc_ref)
    init = tuple((jnp.full((1, tq), NEG_BIG, f32), jnp.zeros((1, tq), f32)) for _ in range(g))
    carry = lax.fori_loop(0, qi, lambda j, c: step(pl.multiple_of(j * tk, tk), c, False), init)
    carry = step(pl.multiple_of(qi * tk, tk), carry, True)
    for hh in range(g):
        o_ref[:, hh * V_DIM:(hh + 1) * V_DIM] = (acc_ref[hh] / carry[hh][1]).T.astype(bf16)


def _mla_attn(q, k, vt):
    batch, heads, seq, _ = q.shape
    tq = tk = 512
    g = MLA_HEADS_PER_STEP
    nq = seq // tq
    return pl.pallas_call(
        functools.partial(_mla_kernel, tq=tq, tk=tk, g=g),
        out_shape=jax.ShapeDtypeStruct((batch * seq, heads * V_DIM), bf16),
        grid=(batch, heads // g, nq),
        in_specs=[pl.BlockSpec((1, g, tq, QK_PAD), lambda b, h, i: (b, h, i, 0)),
                  pl.BlockSpec((1, g, seq, QK_PAD), lambda b, h, i: (b, h, 0, 0)),
                  pl.BlockSpec((1, g, V_DIM, seq), lambda b, h, i: (b, h, 0, 0))],
        out_specs=pl.BlockSpec((tq, g * V_DIM), lambda b, h, i: (b * nq + i, h)),
        scratch_shapes=[pltpu.VMEM((g, V_DIM, tq), f32)],
        compiler_params=_cparams(("arbitrary", "arbitrary", "arbitrary")),
        name="mla_attn",
    )(q, k, vt)


def _sb_kernel(q_ref, k_ref, v_ref, o_ref, acc_ref, *, tq, tk, g):
    qi = pl.program_id(2)
    n_sub = tk // SUB
    ss = lax.broadcasted_iota(jnp.int32, (SUB, SUB), 0)
    jj = lax.broadcasted_iota(jnp.int32, (SUB, SUB), 1)
    later = jnp.where(jj > ss, -1.0, 0.0).astype(bf16)
    later2 = jnp.concatenate([later, later], axis=1)
    head_cols = [slice(hh * SB_DIM, (hh + 1) * SB_DIM) for hh in range(g)]
    qs = [(q_ref[:, cols].astype(f32) * (LOG2_E * SB_DIM ** -0.5)).astype(bf16) for cols in head_cols]

    sub_rows = [slice(sb * SUB, (sb + 1) * SUB) for sb in range(n_sub)]

    def step(off, carry, masked):
        zs = [_dot_nt(k_ref[pl.ds(off, tk), head_cols[hh]], qs[hh]) for hh in range(g)]
        if masked:
            key = lax.broadcasted_iota(jnp.int32, (tk, tq), 0)
            qry = lax.broadcasted_iota(jnp.int32, (tk, tq), 1)
            keep = key < qry
        nloms, log_betas, tails = [], [], []
        for hh in range(g):
            z = zs[hh]
            nlom = jnp.maximum(z, 0.0) + jnp.log(1.0 + jnp.exp2(-jnp.abs(z))) * LOG2_E
            log_betas.append(z - nlom)
            if masked:
                nlom = jnp.where(keep, nlom, 0.0)
            nloms.append(nlom)
            head_tails = []
            for rows in sub_rows:
                nlom_sb = nlom[rows, :]
                hi = nlom_sb.astype(bf16)
                lo = (nlom_sb - hi.astype(f32)).astype(bf16)
                head_tails.append(_dot(later2, jnp.concatenate([hi, lo], axis=0)))
            tails.append(head_tails)
        out = []
        for hh in range(g):
            c = carry[hh]
            a_parts = [None] * n_sub
            for sb in reversed(range(n_sub)):
                rows = sub_rows[sb]
                tail = tails[hh][sb]
                a_sb = jnp.exp2(log_betas[hh][rows, :] + tail + c)
                if masked:
                    a_sb = jnp.where(keep[rows, :], a_sb, 0.0)
                a_parts[sb] = a_sb.astype(bf16)
                c = c + (tail[0:1, :] - nloms[hh][rows, :][0:1, :])
            a = jnp.concatenate(a_parts, axis=0)
            acc_ref[hh] += lax.dot_general(v_ref[pl.ds(off, tk), head_cols[hh]], a, (((0,), (0,)), ((), ())),
                                           preferred_element_type=f32)
            out.append(c)
        return tuple(out)

    def any_live(carry):
        c_max = carry[0]
        for c in carry[1:]:
            c_max = jnp.maximum(c_max, c)
        return jnp.max(c_max) > SB_DEAD_LOG2

    acc_ref[...] = jnp.zeros_like(acc_ref)
    carry = step(pl.multiple_of(qi * tk, tk), tuple(jnp.zeros((1, tq), f32) for _ in range(g)), True)

    def body(state):
        t, carry, _ = state
        carry = step(pl.multiple_of((qi - 1 - t) * tk, tk), carry, False)
        return t + 1, carry, any_live(carry)

    lax.while_loop(lambda s: (s[0] < qi) & s[2], body, (jnp.int32(0), carry, any_live(carry)))
    for hh in range(g):
        o_ref[:, hh * SB_DIM:(hh + 1) * SB_DIM] = acc_ref[hh].T.astype(bf16)


def _sb_attn(proj, batch, seq):
    tq = tk = 256
    g = SB_HEADS_PER_STEP
    w = g * SB_DIM
    nq = seq // tq
    q0, k0, v0 = COL_QSB // w, COL_KSB // w, COL_VSB // w
    return pl.pallas_call(
        functools.partial(_sb_kernel, tq=tq, tk=tk, g=g),
        out_shape=jax.ShapeDtypeStruct((batch * seq, HEADS_W), bf16),
        grid=(batch, N_HEADS // g, nq),
        in_specs=[pl.BlockSpec((tq, w), lambda b, h, i: (b * nq + i, q0 + h)),
                  pl.BlockSpec((seq, w), lambda b, h, i: (b, k0 + h)),
                  pl.BlockSpec((seq, w), lambda b, h, i: (b, v0 + h))],
        out_specs=pl.BlockSpec((tq, w), lambda b, h, i: (b * nq + i, h)),
        scratch_shapes=[pltpu.VMEM((g, SB_DIM, tq), f32)],
        compiler_params=_cparams(("arbitrary", "arbitrary", "arbitrary")),
        name="sb_attn",
    )(proj, proj, proj)


def _merge_out_kernel(ya_ref, yb_ref, ga_ref, gb_ref, x_ref, ada_ref, g2_ref, wa_ref, wb_ref, wo_ref,
                      x1_ref, h2_ref):
    sig = lambda ref: 1.0 / (1.0 + jnp.exp(-ref[...].astype(f32)))
    merged = sig(ga_ref) * _dot(ya_ref[...], wa_ref[...]) + sig(gb_ref) * _dot(yb_ref[...], wb_ref[...])
    x1 = x_ref[...] + ada_ref[0, 2:3, :] * _dot(merged.astype(bf16), wo_ref[...])
    x1_ref[...] = x1
    h2 = _rms_rows(x1, g2_ref[...]) * (1.0 + ada_ref[0, 4:5, :]) + ada_ref[0, 3:4, :]
    h2_ref[...] = h2.astype(bf16)


def _merge_out(ya, yb, proj, x2, ada3, g2, wa, wb, wo, seq):
    t, d = x2.shape
    tm = 256
    per_b = seq // tm
    const = lambda shape: pl.BlockSpec(shape, lambda i: (0, 0), pipeline_mode=pl.Buffered(1))
    return pl.pallas_call(
        _merge_out_kernel,
        out_shape=(jax.ShapeDtypeStruct((t, d), f32), jax.ShapeDtypeStruct((t, d), bf16)),
        grid=(t // tm,),
        in_specs=[pl.BlockSpec((tm, HEADS_W), lambda i: (i, 0)),
                  pl.BlockSpec((tm, HEADS_W), lambda i: (i, 0)),
                  pl.BlockSpec((tm, d), lambda i: (i, COL_GA // d)),
                  pl.BlockSpec((tm, d), lambda i: (i, COL_GB // d)),
                  pl.BlockSpec((tm, d), lambda i: (i, 0)),
                  pl.BlockSpec((1, 6, d), lambda i: (i // per_b, 0, 0)),
                  pl.BlockSpec((1, d), lambda i: (0, 0)),
                  const((HEADS_W, d)), const((HEADS_W, d)), const((d, d))],
        out_specs=(pl.BlockSpec((tm, d), lambda i: (i, 0)), pl.BlockSpec((tm, d), lambda i: (i, 0))),
        compiler_params=_cparams(("arbitrary",)),
        name="merge_out",
    )(ya, yb, proj, proj, x2, ada3, g2, wa, wb, wo)


def _ffn_kernel(h_ref, x1_ref, ada_ref, wg_ref, wu_ref, wo_ref, o_ref, acc_ref):
    f = pl.program_id(1)

    @pl.when(f == 0)
    def _():
        acc_ref[...] = jnp.zeros_like(acc_ref)

    h = h_ref[...]
    gate = _dot(h, wg_ref[...])
    up = _dot(h, wu_ref[...])
    act = gate * (1.0 / (1.0 + jnp.exp(-gate))) * up
    acc_ref[...] += _dot(act.astype(bf16), wo_ref[...])

    @pl.when(f == pl.num_programs(1) - 1)
    def _():
        o_ref[...] = x1_ref[...] + ada_ref[0, 5:6, :] * acc_ref[...]


def _ffn(h2, x1, ada3, w_in, wo, seq):
    t, d = x1.shape
    d_ff = wo.shape[0]
    tm, tf = 512, 512
    per_b = seq // tm
    n_f = d_ff // tf
    return pl.pallas_call(
        _ffn_kernel,
        out_shape=jax.ShapeDtypeStruct((t, d), f32),
        grid=(t // tm, n_f),
        in_specs=[pl.BlockSpec((tm, d), lambda i, f: (i, 0)),
                  pl.BlockSpec((tm, d), lambda i, f: (i, 0)),
                  pl.BlockSpec((1, 6, d), lambda i, f: (i // per_b, 0, 0)),
                  pl.BlockSpec((d, tf), lambda i, f: (0, f)),
                  pl.BlockSpec((d, tf), lambda i, f: (0, n_f + f)),
                  pl.BlockSpec((tf, d), lambda i, f: (f, 0))],
        out_specs=pl.BlockSpec((tm, d), lambda i, f: (i, 0)),
        scratch_shapes=[pltpu.VMEM((tm, d), f32)],
        compiler_params=_cparams(("arbitrary", "arbitrary")),
        name="ffn",
    )(h2, x1, ada3, w_in, w_in, wo)


def _pad_cols(w, width):
    return jnp.pad(w, ((0, 0), (0, width - w.shape[1])))


def _prep_layer(w_in, w_uq, w_ukv, g_q_head, g_k_head, w_proj_mla, w_proj_sb, w_out, w_ffn_in, w_ffn_out):
    o_kpe, o_qsb = 2 * LATENT, 2 * LATENT + ROPE

    def with_swapped(w, axis):
        x1, x2 = jnp.split(w, 2, axis=axis)
        shape = list(w.shape)
        shape[axis] = LANES - ROPE
        z = jnp.zeros(shape, w.dtype)
        return jnp.concatenate([x1, x2, z, x2, x1, z], axis=axis)

    w_lat = w_in[:, :o_kpe].astype(bf16)
    w_rest = w_in[:, o_qsb:].astype(bf16)
    w_kpe = with_swapped(w_in[:, o_kpe:o_qsb], 1).astype(bf16)
    wq = w_uq.reshape(LATENT, N_HEADS, QK_DIM).transpose(1, 0, 2)
    wq_h = jnp.concatenate([wq[..., :NOPE], with_swapped(wq[..., NOPE:], 2)], axis=2).astype(bf16)
    wkv_h = w_ukv.reshape(LATENT, N_HEADS, NOPE + V_DIM).transpose(1, 0, 2).astype(bf16)

    def head_gain(g):
        g1, g2 = g[NOPE:NOPE + HALF], g[NOPE + HALF:]
        z = jnp.zeros((LANES - ROPE,), g.dtype)
        return jnp.concatenate([g[:NOPE], g1, g2, z, -g2, g1, z])[None, :]

    return dict(w_lat=w_lat, w_rest=w_rest, w_kpe=w_kpe, wq_h=wq_h, wkv_h=wkv_h,
                gqh=head_gain(g_q_head), gkh=head_gain(g_k_head),
                wa=w_proj_mla.astype(bf16), wb=w_proj_sb.astype(bf16), wo=w_out.astype(bf16),
                w_ffn=w_ffn_in.astype(bf16), wf=w_ffn_out.astype(bf16))


def kernel(x, c, positions, w_ada, b_ada, g_norm1, g_norm2, w_in, g_q_latent, g_kv_latent, w_uq, w_ukv,
           g_q_head, g_k_head, w_proj_mla, w_proj_sb, w_out, w_ffn_in, w_ffn_out):
    batch, seq, d = x.shape
    depth = w_ada.shape[0]
    t = batch * seq
    x2 = x.reshape(t, d)
    pos2 = positions.reshape(t, 1)
    c_pad = jnp.pad(c, ((0, 8 - batch), (0, 0)))
    lane = np.arange(LANES)
    freq = jnp.where(lane < ROPE, ROPE_THETA ** (-jnp.asarray(lane % HALF, f32) / HALF), 0.0)[None, :]

    for l in range(depth):
        p = _prep_layer(w_in[l], w_uq[l], w_ukv[l], g_q_head[l], g_k_head[l], w_proj_mla[l], w_proj_sb[l],
                        w_out[l], w_ffn_in[l], w_ffn_out[l])
        ada = _ada(c_pad, w_ada[l], b_ada[l][None, :])
        ada3 = ada[:batch].reshape(batch, 6, d)
        proj, kpe = _in_proj(x2, ada3, g_norm1[l][None, :], p["w_lat"], p["w_rest"], p["w_kpe"], seq)
        q, k, vt = _latent_up(proj, kpe, pos2, freq, g_q_latent[l][None, :], g_kv_latent[l][None, :],
                              p["wq_h"], p["wkv_h"], p["gqh"], p["gkh"], batch, seq)
        ya = _mla_attn(q, k, vt)
        yb = _sb_attn(proj, batch, seq)
        x1, h2 = _merge_out(ya, yb, proj, x2, ada3, g_norm2[l][None, :], p["wa"], p["wb"], p["wo"], seq)
        x2 = _ffn(h2, x1, ada3, p["w_ffn"], p["wf"], seq)
    return x2.reshape(batch, seq, d)
```

```python
import functools

import jax
import jax.numpy as jnp
import numpy as np
from jax import lax
from jax.experimental import pallas as pl
from jax.experimental.pallas import tpu as pltpu

D_MODEL = 2048
N_HEADS = 8
LATENT = 512
NOPE = 128
ROPE = 64
HALF = ROPE // 2
QK_DIM = NOPE + ROPE
QK_PAD = 256
V_DIM = 128
SB_DIM = 128
SUB = 128
MLA_HEADS_PER_STEP = 4
SB_HEADS_PER_STEP = 4
HEADS_W = N_HEADS * 128
ROPE_THETA = 10000.0
EPS = 1e-6
LANES = 128
NEG_BIG = -1e30
LOG2_E = 1.4426950408889634
SB_DEAD_LOG2 = -150.0

COL_CQ, COL_CKV, COL_QSB, COL_KSB, COL_VSB, COL_GA, COL_GB = 0, 512, 1024, 2048, 3072, 4096, 6144
PROJ_W = 8192

VMEM_LIMIT = 56 * 1024 * 1024

f32 = jnp.float32
bf16 = jnp.bfloat16


def _cparams(sem):
    return pltpu.CompilerParams(dimension_semantics=sem, vmem_limit_bytes=VMEM_LIMIT)


def _dot(a, b):
    return jnp.dot(a, b, preferred_element_type=f32)


def _dot_nt(a, b):
    return lax.dot_general(a, b, (((1,), (1,)), ((), ())), preferred_element_type=f32)


def _rms_rows(xf, g_row):
    return xf * lax.rsqrt(jnp.mean(xf * xf, axis=-1, keepdims=True) + EPS) * g_row


def _ada_kernel(c_ref, w_ref, b_ref, o_ref):
    c = c_ref[...]
    c_act = (c * (1.0 / (1.0 + jnp.exp(-c)))).astype(bf16)
    o_ref[...] = _dot(c_act, w_ref[...].astype(bf16)) + b_ref[...]


def _ada(c_pad, w_ada, b_ada):
    rows, d = c_pad.shape
    n = w_ada.shape[1]
    tn = 1024
    return pl.pallas_call(
        _ada_kernel,
        out_shape=jax.ShapeDtypeStruct((rows, n), f32),
        grid=(n // tn,),
        in_specs=[pl.BlockSpec((rows, d), lambda j: (0, 0)),
                  pl.BlockSpec((d, tn), lambda j: (0, j)),
                  pl.BlockSpec((1, tn), lambda j: (0, j))],
        out_specs=pl.BlockSpec((rows, tn), lambda j: (0, j)),
        compiler_params=_cparams(("arbitrary",)),
        name="ada",
    )(c_pad, w_ada, b_ada)


def _in_proj_kernel(x_ref, ada_ref, g_ref, wl_ref, w_ref, wk_ref, proj_ref, kpe_ref, h_ref):
    n = pl.program_id(1)

    @pl.when(n == 0)
    def _():
        shift = ada_ref[0, 0:1, :]
        scale = ada_ref[0, 1:2, :]
        h = _rms_rows(x_ref[...], g_ref[...]) * (1.0 + scale) + shift
        h_ref[...] = h.astype(bf16)
        kpe_ref[...] = _dot(h_ref[...], wk_ref[...])
        proj_ref[...] = _dot(h_ref[...], wl_ref[...]).astype(bf16)

    @pl.when(n > 0)
    def _():
        proj_ref[...] = _dot(h_ref[...], w_ref[...]).astype(bf16)


def _in_proj(x2, ada3, g1, w_lat, w_rest, w_kpe, seq):
    t, d = x2.shape
    tm = 1024
    tn = w_lat.shape[1]
    per_b = seq // tm
    return pl.pallas_call(
        _in_proj_kernel,
        out_shape=(jax.ShapeDtypeStruct((t, PROJ_W), bf16), jax.ShapeDtypeStruct((t, 2 * LANES), f32)),
        grid=(t // tm, PROJ_W // tn),
        in_specs=[pl.BlockSpec((tm, d), lambda i, n: (i, 0)),
                  pl.BlockSpec((1, 6, d), lambda i, n: (i // per_b, 0, 0)),
                  pl.BlockSpec((1, d), lambda i, n: (0, 0)),
                  pl.BlockSpec((d, tn), lambda i, n: (0, 0)),
                  pl.BlockSpec((d, tn), lambda i, n: (0, jnp.maximum(n - 1, 0))),
                  pl.BlockSpec((d, 2 * LANES), lambda i, n: (0, 0))],
        out_specs=(pl.BlockSpec((tm, tn), lambda i, n: (i, n)),
                   pl.BlockSpec((tm, 2 * LANES), lambda i, n: (i, 0))),
        scratch_shapes=[pltpu.VMEM((tm, d), bf16)],
        compiler_params=_cparams(("arbitrary", "arbitrary")),
        name="in_proj",
    )(x2, ada3, g1, w_lat, w_rest, w_kpe)


def _row_sumsq(x):
    return _dot((x * x).astype(bf16), jnp.ones((x.shape[1], LANES), bf16))


def _latent_up_kernel(cq_ref, ckv_ref, kpe_ref, pos_ref, freq_ref, gql_ref, gkvl_ref, wq_ref, wkv_ref,
                      gqh_ref, gkh_ref, q_ref, k_ref, vt_ref, cqn_ref, ckvn_ref, cos_ref, sin_ref, kss_ref):
    cqn_ref[...] = _rms_rows(cq_ref[...].astype(f32), gql_ref[...]).astype(bf16)
    ckvn_ref[...] = _rms_rows(ckv_ref[...].astype(f32), gkvl_ref[...]).astype(bf16)
    ang = pos_ref[...].astype(f32) * freq_ref[...]
    cos_ref[...] = jnp.cos(ang)
    sin_ref[...] = jnp.sin(ang)
    kss_ref[...] = _row_sumsq(kpe_ref[:, 0:LANES])
    gq = gqh_ref[...] * (LOG2_E * QK_DIM ** -0.5)
    gk = gkh_ref[...]

    def head(h, _):
        cos_t, sin_t = cos_ref[...], sin_ref[...]
        q = _dot(cqn_ref[...], wq_ref[h])
        q_r = lax.rsqrt(_row_sumsq(q[:, 0:QK_PAD]) * (1.0 / QK_DIM) + EPS)
        q_ref[0, h, :, 0:NOPE] = (q[:, 0:NOPE] * q_r * gq[:, 0:NOPE]).astype(bf16)
        q_ref[0, h, :, NOPE:QK_PAD] = (q[:, NOPE:QK_PAD] * q_r * gq[:, NOPE:QK_PAD] * cos_t
                                       + q[:, QK_PAD:] * q_r * gq[:, QK_PAD:] * sin_t).astype(bf16)

        kv = _dot(ckvn_ref[...], wkv_ref[h])
        k_nope = kv[:, 0:NOPE]
        k_r = lax.rsqrt((_row_sumsq(k_nope) + kss_ref[...]) * (1.0 / QK_DIM) + EPS)
        k_ref[0, h, :, 0:NOPE] = (k_nope * k_r * gk[:, 0:NOPE]).astype(bf16)
        k_ref[0, h, :, NOPE:QK_PAD] = (kpe_ref[:, 0:LANES] * k_r * gk[:, NOPE:QK_PAD] * cos_t
                                       + kpe_ref[:, LANES:] * k_r * gk[:, QK_PAD:] * sin_t).astype(bf16)
        vt_ref[0, h] = kv[:, NOPE:QK_PAD].T.astype(bf16)
        return 0

    lax.fori_loop(0, N_HEADS, head, 0, unroll=2)


def _latent_up(proj, kpe, pos2, freq, gql, gkvl, wq_h, wkv_h, gqh, gkh, batch, seq):
    t = proj.shape[0]
    tm = 512
    per_b = seq // tm
    q_w = wq_h.shape[2]
    heads_out = lambda w: pl.BlockSpec((1, N_HEADS, tm, w), lambda i: (i // per_b, 0, i % per_b, 0))
    row = lambda w: pl.BlockSpec((1, w), lambda i: (0, 0))
    return pl.pallas_call(
        _latent_up_kernel,
        out_shape=(jax.ShapeDtypeStruct((batch, N_HEADS, seq, QK_PAD), bf16),
                   jax.ShapeDtypeStruct((batch, N_HEADS, seq, QK_PAD), bf16),
                   jax.ShapeDtypeStruct((batch, N_HEADS, V_DIM, seq), bf16)),
        grid=(t // tm,),
        in_specs=[pl.BlockSpec((tm, LATENT), lambda i: (i, COL_CQ // LATENT)),
                  pl.BlockSpec((tm, LATENT), lambda i: (i, COL_CKV // LATENT)),
                  pl.BlockSpec((tm, 2 * LANES), lambda i: (i, 0)),
                  pl.BlockSpec((tm, 1), lambda i: (i, 0)),
                  row(LANES), row(LATENT), row(LATENT),
                  pl.BlockSpec((N_HEADS, LATENT, q_w), lambda i: (0, 0, 0)),
                  pl.BlockSpec((N_HEADS, LATENT, QK_PAD), lambda i: (0, 0, 0)),
                  row(q_w), row(q_w)],
        out_specs=(heads_out(QK_PAD), heads_out(QK_PAD),
                   pl.BlockSpec((1, N_HEADS, V_DIM, tm), lambda i: (i // per_b, 0, 0, i % per_b))),
        scratch_shapes=[pltpu.VMEM((tm, LATENT), bf16), pltpu.VMEM((tm, LATENT), bf16),
                        pltpu.VMEM((tm, LANES), f32), pltpu.VMEM((tm, LANES), f32), pltpu.VMEM((tm, LANES), f32)],
        compiler_params=_cparams(("arbitrary",)),
        name="latent_up",
    )(proj, proj, kpe, pos2, freq, gql, gkvl, wq_h, wkv_h, gqh, gkh)


def _mla_kernel(q_ref, k_ref, vt_ref, o_ref, acc_ref, *, tq, tk, g):
    qi = pl.program_id(2)

    def step(off, carry, masked):
        out = []
        scores = [_dot_nt(k_ref[0, hh, pl.ds(off, tk), :], q_ref[0, hh]) for hh in range(g)]
        for hh in range(g):
            m, l = carry[hh]
            s = scores[hh]
            if masked:
                key = lax.broadcasted_iota(jnp.int32, s.shape, 0)
                qry = lax.broadcasted_iota(jnp.int32, s.shape, 1)
                s = jnp.where(key <= qry, s, NEG_BIG)
            m_new = jnp.maximum(m, jnp.max(s, axis=0, keepdims=True))
            p = jnp.exp2(s - m_new)
            alpha = jnp.exp2(m - m_new)
            l_new = alpha * l + jnp.sum(p, axis=0, keepdims=True)
            acc_ref[hh] = alpha * acc_ref[hh] + _dot(vt_ref[0, hh, :, pl.ds(off, tk)], p.astype(bf16))
            out.append((m_new, l_new))
        return tuple(out)

    acc_ref[...] = jnp.zeros_like(acc_ref)
    init = tuple((jnp.full((1, tq), NEG_BIG, f32), jnp.zeros((1, tq), f32)) for _ in range(g))
    carry = lax.fori_loop(0, qi, lambda j, c: step(pl.multiple_of(j * tk, tk), c, False), init)
    carry = step(pl.multiple_of(qi * tk, tk), carry, True)
    for hh in range(g):
        o_ref[:, hh * V_DIM:(hh + 1) * V_DIM] = (acc_ref[hh] / carry[hh][1]).T.astype(bf16)


def _mla_attn(q, k, vt):
    batch, heads, seq, _ = q.shape
    tq = tk = 512
    g = MLA_HEADS_PER_STEP
    nq = seq // tq
    return pl.pallas_call(
        functools.partial(_mla_kernel, tq=tq, tk=tk, g=g),
        out_shape=jax.ShapeDtypeStruct((batch * seq, heads * V_DIM), bf16),
        grid=(batch, heads // g, nq),
        in_specs=[pl.BlockSpec((1, g, tq, QK_PAD), lambda b, h, i: (b, h, i, 0)),
                  pl.BlockSpec((1, g, seq, QK_PAD), lambda b, h, i: (b, h, 0, 0)),
                  pl.BlockSpec((1, g, V_DIM, seq), lambda b, h, i: (b, h, 0, 0))],
        out_specs=pl.BlockSpec((tq, g * V_DIM), lambda b, h, i: (b * nq + i, h)),
        scratch_shapes=[pltpu.VMEM((g, V_DIM, tq), f32)],
        compiler_params=_cparams(("arbitrary", "arbitrary", "arbitrary")),
        name="mla_attn",
    )(q, k, vt)


def _sb_kernel(q_ref, k_ref, v_ref, o_ref, acc_ref, *, tq, tk, g):
    qi = pl.program_id(2)
    n_sub = tk // SUB
    ss = lax.broadcasted_iota(jnp.int32, (SUB, SUB), 0)
    jj = lax.broadcasted_iota(jnp.int32, (SUB, SUB), 1)
    later = jnp.where(jj > ss, -1.0, 0.0).astype(bf16)
    later2 = jnp.concatenate([later, later], axis=1)
    head_cols = [slice(hh * SB_DIM, (hh + 1) * SB_DIM) for hh in range(g)]
    qs = [(q_ref[:, cols].astype(f32) * (LOG2_E * SB_DIM ** -0.5)).astype(bf16) for cols in head_cols]

    sub_rows = [slice(sb * SUB, (sb + 1) * SUB) for sb in range(n_sub)]

    def step(off, carry, masked):
        zs = [_dot_nt(k_ref[pl.ds(off, tk), head_cols[hh]], qs[hh]) for hh in range(g)]
        if masked:
            key = lax.broadcasted_iota(jnp.int32, (tk, tq), 0)
            qry = lax.broadcasted_iota(jnp.int32, (tk, tq), 1)
            keep = key < qry
        nloms, log_betas, tails = [], [], []
        for hh in range(g):
            z = zs[hh]
            nlom = jnp.maximum(z, 0.0) + jnp.log(1.0 + jnp.exp2(-jnp.abs(z))) * LOG2_E
            log_betas.append(z - nlom)
            if masked:
                nlom = jnp.where(keep, nlom, 0.0)
            nloms.append(nlom)
            head_tails = []
            for rows in sub_rows:
                nlom_sb = nlom[rows, :]
                hi = nlom_sb.astype(bf16)
                lo = (nlom_sb - hi.astype(f32)).astype(bf16)
                head_tails.append(_dot(later2, jnp.concatenate([hi, lo], axis=0)))
            tails.append(head_tails)
        out = []
        for hh in range(g):
            c = carry[hh]
            a_parts = [None] * n_sub
            for sb in reversed(range(n_sub)):
                rows = sub_rows[sb]
                tail = tails[hh][sb]
                a_sb = jnp.exp2(log_betas[hh][rows, :] + tail + c)
                if masked:
                    a_sb = jnp.where(keep[rows, :], a_sb, 0.0)
                a_parts[sb] = a_sb.astype(bf16)
                c = c + (tail[0:1, :] - nloms[hh][rows, :][0:1, :])
            a = jnp.concatenate(a_parts, axis=0)
            acc_ref[hh] += lax.dot_general(v_ref[pl.ds(off, tk), head_cols[hh]], a, (((0,), (0,)), ((), ())),
                                           preferred_element_type=f32)
            out.append(c)
        return tuple(out)

    def any_live(carry):
        c_max = carry[0]
        for c in carry[1:]:
            c_max = jnp.maximum(c_max, c)
        return jnp.max(c_max) > SB_DEAD_LOG2

    acc_ref[...] = jnp.zeros_like(acc_ref)
    carry = step(pl.multiple_of(qi * tk, tk), tuple(jnp.zeros((1, tq), f32) for _ in range(g)), True)

    def body(state):
        t, carry, _ = state
        carry = step(pl.multiple_of((qi - 1 - t) * tk, tk), carry, False)
        return t + 1, carry, any_live(carry)

    lax.while_loop(lambda s: (s[0] < qi) & s[2], body, (jnp.int32(0), carry, any_live(carry)))
    for hh in range(g):
        o_ref[:, hh * SB_DIM:(hh + 1) * SB_DIM] = acc_ref[hh].T.astype(bf16)


def _sb_attn(proj, batch, seq):
    tq = tk = 256
    g = SB_HEADS_PER_STEP
    w = g * SB_DIM
    nq = seq // tq
    q0, k0, v0 = COL_QSB // w, COL_KSB // w, COL_VSB // w
    return pl.pallas_call(
        functools.partial(_sb_kernel, tq=tq, tk=tk, g=g),
        out_shape=jax.ShapeDtypeStruct((batch * seq, HEADS_W), bf16),
        grid=(batch, N_HEADS // g, nq),
        in_specs=[pl.BlockSpec((tq, w), lambda b, h, i: (b * nq + i, q0 + h)),
                  pl.BlockSpec((seq, w), lambda b, h, i: (b, k0 + h)),
                  pl.BlockSpec((seq, w), lambda b, h, i: (b, v0 + h))],
        out_specs=pl.BlockSpec((tq, w), lambda b, h, i: (b * nq + i, h)),
        scratch_shapes=[pltpu.VMEM((g, SB_DIM, tq), f32)],
        compiler_params=_cparams(("arbitrary", "arbitrary", "arbitrary")),
        name="sb_attn",
    )(proj, proj, proj)


def _merge_out_kernel(ya_ref, yb_ref, ga_ref, gb_ref, x_ref, ada_ref, g2_ref, wa_ref, wb_ref, wo_ref,
                      x1_ref, h2_ref):
    sig = lambda ref: 1.0 / (1.0 + jnp.exp(-ref[...].astype(f32)))
    merged = sig(ga_ref) * _dot(ya_ref[...], wa_ref[...]) + sig(gb_ref) * _dot(yb_ref[...], wb_ref[...])
    x1 = x_ref[...] + ada_ref[0, 2:3, :] * _dot(merged.astype(bf16), wo_ref[...])
    x1_ref[...] = x1
    h2 = _rms_rows(x1, g2_ref[...]) * (1.0 + ada_ref[0, 4:5, :]) + ada_ref[0, 3:4, :]
    h2_ref[...] = h2.astype(bf16)


def _merge_out(ya, yb, proj, x2, ada3, g2, wa, wb, wo, seq):
    t, d = x2.shape
    tm = 256
    per_b = seq // tm
    const = lambda shape: pl.BlockSpec(shape, lambda i: (0, 0), pipeline_mode=pl.Buffered(1))
    return pl.pallas_call(
        _merge_out_kernel,
        out_shape=(jax.ShapeDtypeStruct((t, d), f32), jax.ShapeDtypeStruct((t, d), bf16)),
        grid=(t // tm,),
        in_specs=[pl.BlockSpec((tm, HEADS_W), lambda i: (i, 0)),
                  pl.BlockSpec((tm, HEADS_W), lambda i: (i, 0)),
                  pl.BlockSpec((tm, d), lambda i: (i, COL_GA // d)),
                  pl.BlockSpec((tm, d), lambda i: (i, COL_GB // d)),
                  pl.BlockSpec((tm, d), lambda i: (i, 0)),
                  pl.BlockSpec((1, 6, d), lambda i: (i // per_b, 0, 0)),
                  pl.BlockSpec((1, d), lambda i: (0, 0)),
                  const((HEADS_W, d)), const((HEADS_W, d)), const((d, d))],
        out_specs=(pl.BlockSpec((tm, d), lambda i: (i, 0)), pl.BlockSpec((tm, d), lambda i: (i, 0))),
        compiler_params=_cparams(("arbitrary",)),
        name="merge_out",
    )(ya, yb, proj, proj, x2, ada3, g2, wa, wb, wo)


def _ffn_kernel(h_ref, x1_ref, ada_ref, wg_ref, wu_ref, wo_ref, o_ref, acc_ref):
    f = pl.program_id(1)

    @pl.when(f == 0)
    def _():
        acc_ref[...] = jnp.zeros_like(acc_ref)

    h = h_ref[...]
    gate = _dot(h, wg_ref[...])
    up = _dot(h, wu_ref[...])
    act = gate * (1.0 / (1.0 + jnp.exp(-gate))) * up
    acc_ref[...] += _dot(act.astype(bf16), wo_ref[...])

    @pl.when(f == pl.num_programs(1) - 1)
    def _():
        o_ref[...] = x1_ref[...] + ada_ref[0, 5:6, :] * acc_ref[...]


def _ffn(h2, x1, ada3, w_in, wo, seq):
    t, d = x1.shape
    d_ff = wo.shape[0]
    tm, tf = 512, 512
    per_b = seq // tm
    n_f = d_ff // tf
    return pl.pallas_call(
        _ffn_kernel,
        out_shape=jax.ShapeDtypeStruct((t, d), f32),
        grid=(t // tm, n_f),
        in_specs=[pl.BlockSpec((tm, d), lambda i, f: (i, 0)),
                  pl.BlockSpec((tm, d), lambda i, f: (i, 0)),
                  pl.BlockSpec((1, 6, d), lambda i, f: (i // per_b, 0, 0)),
                  pl.BlockSpec((d, tf), lambda i, f: (0, f)),
                  pl.BlockSpec((d, tf), lambda i, f: (0, n_f + f)),
                  pl.BlockSpec((tf, d), lambda i, f: (f, 0))],
        out_specs=pl.BlockSpec((tm, d), lambda i, f: (i, 0)),
        scratch_shapes=[pltpu.VMEM((tm, d), f32)],
        compiler_params=_cparams(("arbitrary", "arbitrary")),
        name="ffn",
    )(h2, x1, ada3, w_in, w_in, wo)


def _pad_cols(w, width):
    return jnp.pad(w, ((0, 0), (0, width - w.shape[1])))


def _prep_layer(w_in, w_uq, w_ukv, g_q_head, g_k_head, w_proj_mla, w_proj_sb, w_out, w_ffn_in, w_ffn_out):
    o_kpe, o_qsb = 2 * LATENT, 2 * LATENT + ROPE

    def with_swapped(w, axis):
        x1, x2 = jnp.split(w, 2, axis=axis)
        shape = list(w.shape)
        shape[axis] = LANES - ROPE
        z = jnp.zeros(shape, w.dtype)
        return jnp.concatenate([x1, x2, z, x2, x1, z], axis=axis)

    w_lat = w_in[:, :o_kpe].astype(bf16)
    w_rest = w_in[:, o_qsb:].astype(bf16)
    w_kpe = with_swapped(w_in[:, o_kpe:o_qsb], 1).astype(bf16)
    wq = w_uq.reshape(LATENT, N_HEADS, QK_DIM).transpose(1, 0, 2)
    wq_h = jnp.concatenate([wq[..., :NOPE], with_swapped(wq[..., NOPE:], 2)], axis=2).astype(bf16)
    wkv_h = w_ukv.reshape(LATENT, N_HEADS, NOPE + V_DIM).transpose(1, 0, 2).astype(bf16)

    def head_gain(g):
        g1, g2 = g[NOPE:NOPE + HALF], g[NOPE + HALF:]
        z = jnp.zeros((LANES - ROPE,), g.dtype)
        return jnp.concatenate([g[:NOPE], g1, g2, z, -g2, g1, z])[None, :]

    return dict(w_lat=w_lat, w_rest=w_rest, w_kpe=w_kpe, wq_h=wq_h, wkv_h=wkv_h,
                gqh=head_gain(g_q_head), gkh=head_gain(g_k_head),
                wa=w_proj_mla.astype(bf16), wb=w_proj_sb.astype(bf16), wo=w_out.astype(bf16),
                w_ffn=w_ffn_in.astype(bf16), wf=w_ffn_out.astype(bf16))


def kernel(x, c, positions, w_ada, b_ada, g_norm1, g_norm2, w_in, g_q_latent, g_kv_latent, w_uq, w_ukv,
           g_q_head, g_k_head, w_proj_mla, w_proj_sb, w_out, w_ffn_in, w_ffn_out):
    batch, seq, d = x.shape
    depth = w_ada.shape[0]
    t = batch * seq
    x2 = x.reshape(t, d)
    pos2 = positions.reshape(t, 1)
    c_pad = jnp.pad(c, ((0, 8 - batch), (0, 0)))
    lane = np.arange(LANES)
    freq = jnp.where(lane < ROPE, ROPE_THETA ** (-jnp.asarray(lane % HALF, f32) / HALF), 0.0)[None, :]

    for l in range(depth):
        p = _prep_layer(w_in[l], w_uq[l], w_ukv[l], g_q_head[l], g_k_head[l], w_proj_mla[l], w_proj_sb[l],
                        w_out[l], w_ffn_in[l], w_ffn_out[l])
        ada = _ada(c_pad, w_ada[l], b_ada[l][None, :])
        ada3 = ada[:batch].reshape(batch, 6, d)
        proj, kpe = _in_proj(x2, ada3, g_norm1[l][None, :], p["w_lat"], p["w_rest"], p["w_kpe"], seq)
        q, k, vt = _latent_up(proj, kpe, pos2, freq, g_q_latent[l][None, :], g_kv_latent[l][None, :],
                              p["wq_h"], p["wkv_h"], p["gqh"], p["gkh"], batch, seq)
        ya = _mla_attn(q, k, vt)
        yb = _sb_attn(proj, batch, seq)
        x1, h2 = _merge_out(ya, yb, proj, x2, ada3, g_norm2[l][None, :], p["wa"], p["wb"], p["wo"], seq)
        x2 = _ffn(h2, x1, ada3, p["w_ffn"], p["wf"], seq)
    return x2.reshape(batch, seq, d)
```

```python
import functools

import jax
import jax.numpy as jnp
import numpy as np
from jax import lax
from jax.experimental import pallas as pl
from jax.experimental.pallas import tpu as pltpu

D_MODEL = 2048
N_HEADS = 8
LATENT = 512
NOPE = 128
ROPE = 64
HALF = ROPE // 2
QK_DIM = NOPE + ROPE
QK_PAD = 256
V_DIM = 128
SB_DIM = 128
SUB = 128
MLA_HEADS_PER_STEP = 4
SB_HEADS_PER_STEP = 4
HEADS_W = N_HEADS * 128
ROPE_THETA = 10000.0
EPS = 1e-6
LANES = 128
NEG_BIG = -1e30
LOG2_E = 1.4426950408889634
SB_DEAD_LOG2 = -150.0
SHIFT_MAX = 60.0

COL_CQ, COL_CKV, COL_QSB, COL_KSB, COL_VSB, COL_GA, COL_GB = 0, 512, 1024, 2048, 3072, 4096, 6144
PROJ_W = 8192

VMEM_LIMIT = 56 * 1024 * 1024

f32 = jnp.float32
bf16 = jnp.bfloat16


def _cparams(sem):
    return pltpu.CompilerParams(dimension_semantics=sem, vmem_limit_bytes=VMEM_LIMIT)


def _dot(a, b):
    return jnp.dot(a, b, preferred_element_type=f32)


def _dot_nt(a, b):
    return lax.dot_general(a, b, (((1,), (1,)), ((), ())), preferred_element_type=f32)


def _rms_rows(xf, g_row):
    return xf * lax.rsqrt(jnp.mean(xf * xf, axis=-1, keepdims=True) + EPS) * g_row


def _ada_kernel(c_ref, w_ref, b_ref, o_ref):
    c = c_ref[...]
    c_act = (c * (1.0 / (1.0 + jnp.exp(-c)))).astype(bf16)
    o_ref[...] = _dot(c_act, w_ref[...].astype(bf16)) + b_ref[...]


def _ada(c_pad, w_ada, b_ada):
    rows, d = c_pad.shape
    n = w_ada.shape[1]
    tn = 1024
    return pl.pallas_call(
        _ada_kernel,
        out_shape=jax.ShapeDtypeStruct((rows, n), f32),
        grid=(n // tn,),
        in_specs=[pl.BlockSpec((rows, d), lambda j: (0, 0)),
                  pl.BlockSpec((d, tn), lambda j: (0, j)),
                  pl.BlockSpec((1, tn), lambda j: (0, j))],
        out_specs=pl.BlockSpec((rows, tn), lambda j: (0, j)),
        compiler_params=_cparams(("arbitrary",)),
        name="ada",
    )(c_pad, w_ada, b_ada)


def _in_proj_kernel(x_ref, ada_ref, g_ref, wl_ref, w_ref, wk_ref, proj_ref, kpe_ref, h_ref):
    n = pl.program_id(1)

    @pl.when(n == 0)
    def _():
        shift = ada_ref[0, 0:1, :]
        scale = ada_ref[0, 1:2, :]
        h = _rms_rows(x_ref[...], g_ref[...]) * (1.0 + scale) + shift
        h_ref[...] = h.astype(bf16)
        kpe_ref[...] = _dot(h_ref[...], wk_ref[...])
        proj_ref[...] = _dot(h_ref[...], wl_ref[...]).astype(bf16)

    @pl.when(n > 0)
    def _():
        proj_ref[...] = _dot(h_ref[...], w_ref[...]).astype(bf16)


def _in_proj(x2, ada3, g1, w_lat, w_rest, w_kpe, seq):
    t, d = x2.shape
    tm = 1024
    tn = w_lat.shape[1]
    per_b = seq // tm
    return pl.pallas_call(
        _in_proj_kernel,
        out_shape=(jax.ShapeDtypeStruct((t, PROJ_W), bf16), jax.ShapeDtypeStruct((t, 2 * LANES), f32)),
        grid=(t // tm, PROJ_W // tn),
        in_specs=[pl.BlockSpec((tm, d), lambda i, n: (i, 0)),
                  pl.BlockSpec((1, 6, d), lambda i, n: (i // per_b, 0, 0)),
                  pl.BlockSpec((1, d), lambda i, n: (0, 0)),
                  pl.BlockSpec((d, tn), lambda i, n: (0, 0)),
                  pl.BlockSpec((d, tn), lambda i, n: (0, jnp.maximum(n - 1, 0))),
                  pl.BlockSpec((d, 2 * LANES), lambda i, n: (0, 0))],
        out_specs=(pl.BlockSpec((tm, tn), lambda i, n: (i, n)),
                   pl.BlockSpec((tm, 2 * LANES), lambda i, n: (i, 0))),
        scratch_shapes=[pltpu.VMEM((tm, d), bf16)],
        compiler_params=_cparams(("arbitrary", "arbitrary")),
        name="in_proj",
    )(x2, ada3, g1, w_lat, w_rest, w_kpe)


def _row_sumsq(x):
    return _dot((x * x).astype(bf16), jnp.ones((x.shape[1], LANES), bf16))


def _latent_up_kernel(cq_ref, ckv_ref, kpe_ref, pos_ref, freq_ref, gql_ref, gkvl_ref, wq_ref, wkv_ref,
                      gqh_ref, gkh_ref, padq_ref, padk_ref, q_ref, k_ref, vt_ref,
                      cqn_ref, ckvn_ref, cos_ref, sin_ref, kss_ref):
    cqn_ref[...] = _rms_rows(cq_ref[...].astype(f32), gql_ref[...]).astype(bf16)
    ckvn_ref[...] = _rms_rows(ckv_ref[...].astype(f32), gkvl_ref[...]).astype(bf16)
    ang = pos_ref[...].astype(f32) * freq_ref[...]
    cos_ref[...] = jnp.cos(ang)
    sin_ref[...] = jnp.sin(ang)
    kss_ref[...] = _row_sumsq(kpe_ref[:, 0:LANES])
    gq = gqh_ref[...] * (LOG2_E * QK_DIM ** -0.5)
    gk = gkh_ref[...]

    def head(h, _):
        cos_t, sin_t = cos_ref[...], sin_ref[...]
        q = _dot(cqn_ref[...], wq_ref[h])
        q_r = lax.rsqrt(_row_sumsq(q[:, 0:QK_PAD]) * (1.0 / QK_DIM) + EPS)
        q_ref[0, h, :, 0:NOPE] = (q[:, 0:NOPE] * q_r * gq[:, 0:NOPE]).astype(bf16)
        q_ref[0, h, :, NOPE:QK_PAD] = (q[:, NOPE:QK_PAD] * q_r * gq[:, NOPE:QK_PAD] * cos_t
                                       + q[:, QK_PAD:] * q_r * gq[:, QK_PAD:] * sin_t + padq_ref[...]).astype(bf16)

        kv = _dot(ckvn_ref[...], wkv_ref[h])
        k_nope = kv[:, 0:NOPE]
        k_r = lax.rsqrt((_row_sumsq(k_nope) + kss_ref[...]) * (1.0 / QK_DIM) + EPS)
        k_ref[0, h, :, 0:NOPE] = (k_nope * k_r * gk[:, 0:NOPE]).astype(bf16)
        k_ref[0, h, :, NOPE:QK_PAD] = (kpe_ref[:, 0:LANES] * k_r * gk[:, NOPE:QK_PAD] * cos_t
                                       + kpe_ref[:, LANES:] * k_r * gk[:, QK_PAD:] * sin_t + padk_ref[...]).astype(bf16)
        vt_ref[0, h] = kv[:, NOPE:QK_PAD].T.astype(bf16)
        return 0

    lax.fori_loop(0, N_HEADS, head, 0, unroll=2)


def _latent_up(proj, kpe, pos2, freq, gql, gkvl, wq_h, wkv_h, gqh, gkh, padq, padk, batch, seq):
    t = proj.shape[0]
    tm = 512
    per_b = seq // tm
    q_w = wq_h.shape[2]
    heads_out = lambda w: pl.BlockSpec((1, N_HEADS, tm, w), lambda i: (i // per_b, 0, i % per_b, 0))
    row = lambda w: pl.BlockSpec((1, w), lambda i: (0, 0))
    return pl.pallas_call(
        _latent_up_kernel,
        out_shape=(jax.ShapeDtypeStruct((batch, N_HEADS, seq, QK_PAD), bf16),
                   jax.ShapeDtypeStruct((batch, N_HEADS, seq, QK_PAD), bf16),
                   jax.ShapeDtypeStruct((batch, N_HEADS, V_DIM, seq), bf16)),
        grid=(t // tm,),
        in_specs=[pl.BlockSpec((tm, LATENT), lambda i: (i, COL_CQ // LATENT)),
                  pl.BlockSpec((tm, LATENT), lambda i: (i, COL_CKV // LATENT)),
                  pl.BlockSpec((tm, 2 * LANES), lambda i: (i, 0)),
                  pl.BlockSpec((tm, 1), lambda i: (i, 0)),
                  row(LANES), row(LATENT), row(LATENT),
                  pl.BlockSpec((N_HEADS, LATENT, q_w), lambda i: (0, 0, 0)),
                  pl.BlockSpec((N_HEADS, LATENT, QK_PAD), lambda i: (0, 0, 0)),
                  row(q_w), row(q_w), row(LANES), row(LANES)],
        out_specs=(heads_out(QK_PAD), heads_out(QK_PAD),
                   pl.BlockSpec((1, N_HEADS, V_DIM, tm), lambda i: (i // per_b, 0, 0, i % per_b))),
        scratch_shapes=[pltpu.VMEM((tm, LATENT), bf16), pltpu.VMEM((tm, LATENT), bf16),
                        pltpu.VMEM((tm, LANES), f32), pltpu.VMEM((tm, LANES), f32), pltpu.VMEM((tm, LANES), f32)],
        compiler_params=_cparams(("arbitrary",)),
        name="latent_up",
    )(proj, proj, kpe, pos2, freq, gql, gkvl, wq_h, wkv_h, gqh, gkh, padq, padk)


def _mla_kernel(q_ref, k_ref, vt_ref, o_ref, acc_ref, *, tq, tk, g):
    qi = pl.program_id(2)

    def step(off, carry, masked):
        out = []
        scores = [_dot_nt(k_ref[0, hh, pl.ds(off, tk), :], q_ref[0, hh]) for hh in range(g)]
        for hh in range(g):
            m, l = carry[hh]
            s = scores[hh]
            if masked:
                key = lax.broadcasted_iota(jnp.int32, s.shape, 0)
                qry = lax.broadcasted_iota(jnp.int32, s.shape, 1)
                s = jnp.where(key <= qry, s, NEG_BIG)
            m_new = jnp.maximum(m, jnp.max(s, axis=0, keepdims=True))
            p = jnp.exp2(s - m_new)
            alpha = jnp.exp2(m - m_new)
            l_new = alpha * l + jnp.sum(p, axis=0, keepdims=True)
            acc_ref[hh] = alpha * acc_ref[hh] + _dot(vt_ref[0, hh, :, pl.ds(off, tk)], p.astype(bf16))
            out.append((m_new, l_new))
        return tuple(out)

    acc_ref[...] = jnp.zeros_like(acc_ref)
    init = tuple((jnp.full((1, tq), NEG_BIG, f32), jnp.zeros((1, tq), f32)) for _ in range(g))
    carry = lax.fori_loop(0, qi, lambda j, c: step(pl.multiple_of(j * tk, tk), c, False), init)
    carry = step(pl.multiple_of(qi * tk, tk), carry, True)
    for hh in range(g):
        o_ref[:, hh * V_DIM:(hh + 1) * V_DIM] = (acc_ref[hh] / carry[hh][1]).T.astype(bf16)


def _mla_shifted_kernel(q_ref, k_ref, vt_ref, o_ref, acc_ref, *, tq, tk, g):
    qi = pl.program_id(2)

    def step(off, ls, masked):
        out = []
        scores = [_dot_nt(k_ref[0, hh, pl.ds(off, tk), :], q_ref[0, hh]) for hh in range(g)]
        for hh in range(g):
            s = scores[hh]
            if masked:
                key = lax.broadcasted_iota(jnp.int32, s.shape, 0)
                qry = lax.broadcasted_iota(jnp.int32, s.shape, 1)
                s = jnp.where(key <= qry, s, NEG_BIG)
            p = jnp.exp2(s)
            acc_ref[hh] += _dot(vt_ref[0, hh, :, pl.ds(off, tk)], p.astype(bf16))
            out.append(ls[hh] + jnp.sum(p, axis=0, keepdims=True))
        return tuple(out)

    acc_ref[...] = jnp.zeros_like(acc_ref)
    init = tuple(jnp.zeros((1, tq), f32) for _ in range(g))
    ls = lax.fori_loop(0, qi, lambda j, c: step(pl.multiple_of(j * tk, tk), c, False), init)
    ls = step(pl.multiple_of(qi * tk, tk), ls, True)
    for hh in range(g):
        o_ref[:, hh * V_DIM:(hh + 1) * V_DIM] = (acc_ref[hh] / ls[hh]).T.astype(bf16)


def _mla_attn(q, k, vt, shifted):
    batch, heads, seq, _ = q.shape
    tq = tk = 512
    g = MLA_HEADS_PER_STEP
    nq = seq // tq
    body = _mla_shifted_kernel if shifted else _mla_kernel
    return pl.pallas_call(
        functools.partial(body, tq=tq, tk=tk, g=g),
        out_shape=jax.ShapeDtypeStruct((batch * seq, heads * V_DIM), bf16),
        grid=(batch, heads // g, nq),
        in_specs=[pl.BlockSpec((1, g, tq, QK_PAD), lambda b, h, i: (b, h, i, 0)),
                  pl.BlockSpec((1, g, seq, QK_PAD), lambda b, h, i: (b, h, 0, 0)),
                  pl.BlockSpec((1, g, V_DIM, seq), lambda b, h, i: (b, h, 0, 0))],
        out_specs=pl.BlockSpec((tq, g * V_DIM), lambda b, h, i: (b * nq + i, h)),
        scratch_shapes=[pltpu.VMEM((g, V_DIM, tq), f32)],
        compiler_params=_cparams(("arbitrary", "arbitrary", "arbitrary")),
        name="mla_attn_shifted" if shifted else "mla_attn",
    )(q, k, vt)


def _sb_kernel(q_ref, k_ref, v_ref, o_ref, acc_ref, *, tq, tk, g):
    qi = pl.program_id(2)
    n_sub = tk // SUB
    ss = lax.broadcasted_iota(jnp.int32, (SUB, SUB), 0)
    jj = lax.broadcasted_iota(jnp.int32, (SUB, SUB), 1)
    later = jnp.where(jj > ss, -1.0, 0.0).astype(bf16)
    later2 = jnp.concatenate([later, later], axis=1)
    head_cols = [slice(hh * SB_DIM, (hh + 1) * SB_DIM) for hh in range(g)]
    qs = [(q_ref[:, cols].astype(f32) * (LOG2_E * SB_DIM ** -0.5)).astype(bf16) for cols in head_cols]

    sub_rows = [slice(sb * SUB, (sb + 1) * SUB) for sb in range(n_sub)]

    def step(off, carry, masked):
        zs = [_dot_nt(k_ref[pl.ds(off, tk), head_cols[hh]], qs[hh]) for hh in range(g)]
        if masked:
            key = lax.broadcasted_iota(jnp.int32, (tk, tq), 0)
            qry = lax.broadcasted_iota(jnp.int32, (tk, tq), 1)
            keep = key < qry
        nloms, log_betas, tails = [], [], []
        for hh in range(g):
            z = zs[hh]
            nlom = jnp.maximum(z, 0.0) + jnp.log(1.0 + jnp.exp2(-jnp.abs(z))) * LOG2_E
            log_betas.append(z - nlom)
            if masked:
                nlom = jnp.where(keep, nlom, 0.0)
            nloms.append(nlom)
            head_tails = []
            for rows in sub_rows:
                nlom_sb = nlom[rows, :]
                hi = nlom_sb.astype(bf16)
                lo = (nlom_sb - hi.astype(f32)).astype(bf16)
                head_tails.append(_dot(later2, jnp.concatenate([hi, lo], axis=0)))
            tails.append(head_tails)
        out = []
        for hh in range(g):
            c = carry[hh]
            a_parts = [None] * n_sub
            for sb in reversed(range(n_sub)):
                rows = sub_rows[sb]
                tail = tails[hh][sb]
                a_sb = jnp.exp2(log_betas[hh][rows, :] + tail + c)
                if masked:
                    a_sb = jnp.where(keep[rows, :], a_sb, 0.0)
                a_parts[sb] = a_sb.astype(bf16)
                c = c + (tail[0:1, :] - nloms[hh][rows, :][0:1, :])
            a = jnp.concatenate(a_parts, axis=0)
            acc_ref[hh] += lax.dot_general(v_ref[pl.ds(off, tk), head_cols[hh]], a, (((0,), (0,)), ((), ())),
                                           preferred_element_type=f32)
            out.append(c)
        return tuple(out)

    def any_live(carry):
        c_max = carry[0]
        for c in carry[1:]:
            c_max = jnp.maximum(c_max, c)
        return jnp.max(c_max) > SB_DEAD_LOG2

    acc_ref[...] = jnp.zeros_like(acc_ref)
    carry = step(pl.multiple_of(qi * tk, tk), tuple(jnp.zeros((1, tq), f32) for _ in range(g)), True)

    def body(state):
        t, carry, _ = state
        carry = step(pl.multiple_of((qi - 1 - t) * tk, tk), carry, False)
        return t + 1, carry, any_live(carry)

    lax.while_loop(lambda s: (s[0] < qi) & s[2], body, (jnp.int32(0), carry, any_live(carry)))
    for hh in range(g):
        o_ref[:, hh * SB_DIM:(hh + 1) * SB_DIM] = acc_ref[hh].T.astype(bf16)


def _sb_attn(proj, batch, seq):
    tq = tk = 256
    g = SB_HEADS_PER_STEP
    w = g * SB_DIM
    nq = seq // tq
    q0, k0, v0 = COL_QSB // w, COL_KSB // w, COL_VSB // w
    return pl.pallas_call(
        functools.partial(_sb_kernel, tq=tq, tk=tk, g=g),
        out_shape=jax.ShapeDtypeStruct((batch * seq, HEADS_W), bf16),
        grid=(batch, N_HEADS // g, nq),
        in_specs=[pl.BlockSpec((tq, w), lambda b, h, i: (b * nq + i, q0 + h)),
                  pl.BlockSpec((seq, w), lambda b, h, i: (b, k0 + h)),
                  pl.BlockSpec((seq, w), lambda b, h, i: (b, v0 + h))],
        out_specs=pl.BlockSpec((tq, w), lambda b, h, i: (b * nq + i, h)),
        scratch_shapes=[pltpu.VMEM((g, SB_DIM, tq), f32)],
        compiler_params=_cparams(("arbitrary", "arbitrary", "arbitrary")),
        name="sb_attn",
    )(proj, proj, proj)


def _merge_out_kernel(ya_ref, yb_ref, ga_ref, gb_ref, x_ref, ada_ref, g2_ref, wa_ref, wb_ref, wo_ref,
                      x1_ref, h2_ref):
    sig = lambda ref: 1.0 / (1.0 + jnp.exp(-ref[...].astype(f32)))
    merged = sig(ga_ref) * _dot(ya_ref[...], wa_ref[...]) + sig(gb_ref) * _dot(yb_ref[...], wb_ref[...])
    x1 = x_ref[...] + ada_ref[0, 2:3, :] * _dot(merged.astype(bf16), wo_ref[...])
    x1_ref[...] = x1
    h2 = _rms_rows(x1, g2_ref[...]) * (1.0 + ada_ref[0, 4:5, :]) + ada_ref[0, 3:4, :]
    h2_ref[...] = h2.astype(bf16)


def _merge_out(ya, yb, proj, x2, ada3, g2, wa, wb, wo, seq):
    t, d = x2.shape
    tm = 256
    per_b = seq // tm
    const = lambda shape: pl.BlockSpec(shape, lambda i: (0, 0), pipeline_mode=pl.Buffered(1))
    return pl.pallas_call(
        _merge_out_kernel,
        out_shape=(jax.ShapeDtypeStruct((t, d), f32), jax.ShapeDtypeStruct((t, d), bf16)),
        grid=(t // tm,),
        in_specs=[pl.BlockSpec((tm, HEADS_W), lambda i: (i, 0)),
                  pl.BlockSpec((tm, HEADS_W), lambda i: (i, 0)),
                  pl.BlockSpec((tm, d), lambda i: (i, COL_GA // d)),
                  pl.BlockSpec((tm, d), lambda i: (i, COL_GB // d)),
                  pl.BlockSpec((tm, d), lambda i: (i, 0)),
                  pl.BlockSpec((1, 6, d), lambda i: (i // per_b, 0, 0)),
                  pl.BlockSpec((1, d), lambda i: (0, 0)),
                  const((HEADS_W, d)), const((HEADS_W, d)), const((d, d))],
        out_specs=(pl.BlockSpec((tm, d), lambda i: (i, 0)), pl.BlockSpec((tm, d), lambda i: (i, 0))),
        compiler_params=_cparams(("arbitrary",)),
        name="merge_out",
    )(ya, yb, proj, proj, x2, ada3, g2, wa, wb, wo)


def _ffn_kernel(h_ref, x1_ref, ada_ref, wg_ref, wu_ref, wo_ref, o_ref, acc_ref):
    f = pl.program_id(1)

    @pl.when(f == 0)
    def _():
        acc_ref[...] = jnp.zeros_like(acc_ref)

    h = h_ref[...]
    gate = _dot(h, wg_ref[...])
    up = _dot(h, wu_ref[...])
    act = gate * (1.0 / (1.0 + jnp.exp(-gate))) * up
    acc_ref[...] += _dot(act.astype(bf16), wo_ref[...])

    @pl.when(f == pl.num_programs(1) - 1)
    def _():
        o_ref[...] = x1_ref[...] + ada_ref[0, 5:6, :] * acc_ref[...]


def _ffn(h2, x1, ada3, w_in, wo, seq):
    t, d = x1.shape
    d_ff = wo.shape[0]
    tm, tf = 512, 512
    per_b = seq // tm
    n_f = d_ff // tf
    return pl.pallas_call(
        _ffn_kernel,
        out_shape=jax.ShapeDtypeStruct((t, d), f32),
        grid=(t // tm, n_f),
        in_specs=[pl.BlockSpec((tm, d), lambda i, f: (i, 0)),
                  pl.BlockSpec((tm, d), lambda i, f: (i, 0)),
                  pl.BlockSpec((1, 6, d), lambda i, f: (i // per_b, 0, 0)),
                  pl.BlockSpec((d, tf), lambda i, f: (0, f)),
                  pl.BlockSpec((d, tf), lambda i, f: (0, n_f + f)),
                  pl.BlockSpec((tf, d), lambda i, f: (f, 0))],
        out_specs=pl.BlockSpec((tm, d), lambda i, f: (i, 0)),
        scratch_shapes=[pltpu.VMEM((tm, d), f32)],
        compiler_params=_cparams(("arbitrary", "arbitrary")),
        name="ffn",
    )(h2, x1, ada3, w_in, w_in, wo)


def _score_shift(g_q_head, g_k_head):
    c = LOG2_E * QK_DIM ** -0.5
    return 1.02 * QK_DIM * c * jnp.max(jnp.abs(g_q_head)) * jnp.max(jnp.abs(g_k_head))


def _prep_layer(w_in, w_uq, w_ukv, g_q_head, g_k_head, w_proj_mla, w_proj_sb, w_out, w_ffn_in, w_ffn_out):
    o_kpe, o_qsb = 2 * LATENT, 2 * LATENT + ROPE

    def with_swapped(w, axis):
        x1, x2 = jnp.split(w, 2, axis=axis)
        shape = list(w.shape)
        shape[axis] = LANES - ROPE
        z = jnp.zeros(shape, w.dtype)
        return jnp.concatenate([x1, x2, z, x2, x1, z], axis=axis)

    w_lat = w_in[:, :o_kpe].astype(bf16)
    w_rest = w_in[:, o_qsb:].astype(bf16)
    w_kpe = with_swapped(w_in[:, o_kpe:o_qsb], 1).astype(bf16)
    wq = w_uq.reshape(LATENT, N_HEADS, QK_DIM).transpose(1, 0, 2)
    wq_h = jnp.concatenate([wq[..., :NOPE], with_swapped(wq[..., NOPE:], 2)], axis=2).astype(bf16)
    wkv_h = w_ukv.reshape(LATENT, N_HEADS, NOPE + V_DIM).transpose(1, 0, 2).astype(bf16)

    def head_gain(g):
        g1, g2 = g[NOPE:NOPE + HALF], g[NOPE + HALF:]
        z = jnp.zeros((LANES - ROPE,), g.dtype)
        return jnp.concatenate([g[:NOPE], g1, g2, z, -g2, g1, z])[None, :]

    return dict(w_lat=w_lat, w_rest=w_rest, w_kpe=w_kpe, wq_h=wq_h, wkv_h=wkv_h,
                gqh=head_gain(g_q_head), gkh=head_gain(g_k_head),
                wa=w_proj_mla.astype(bf16), wb=w_proj_sb.astype(bf16), wo=w_out.astype(bf16),
                w_ffn=w_ffn_in.astype(bf16), wf=w_ffn_out.astype(bf16))


def kernel(x, c, positions, w_ada, b_ada, g_norm1, g_norm2, w_in, g_q_latent, g_kv_latent, w_uq, w_ukv,
           g_q_head, g_k_head, w_proj_mla, w_proj_sb, w_out, w_ffn_in, w_ffn_out):
    batch, seq, d = x.shape
    depth = w_ada.shape[0]
    t = batch * seq
    x2 = x.reshape(t, d)
    pos2 = positions.reshape(t, 1)
    c_pad = jnp.pad(c, ((0, 8 - batch), (0, 0)))
    lane = np.arange(LANES)
    freq = jnp.where(lane < ROPE, ROPE_THETA ** (-jnp.asarray(lane % HALF, f32) / HALF), 0.0)[None, :]

    for l in range(depth):
        p = _prep_layer(w_in[l], w_uq[l], w_ukv[l], g_q_head[l], g_k_head[l], w_proj_mla[l], w_proj_sb[l],
                        w_out[l], w_ffn_in[l], w_ffn_out[l])
        ada = _ada(c_pad, w_ada[l], b_ada[l][None, :])
        ada3 = ada[:batch].reshape(batch, 6, d)
        proj, kpe = _in_proj(x2, ada3, g_norm1[l][None, :], p["w_lat"], p["w_rest"], p["w_kpe"], seq)
        shift = _score_shift(g_q_head[l], g_k_head[l])
        use_shift = shift <= SHIFT_MAX
        pad_lane = (lane == ROPE).astype(f32)[None, :]
        padq = pad_lane * use_shift.astype(f32)
        padk = pad_lane * jnp.where(use_shift, -shift, 0.0)
        q, k, vt = _latent_up(proj, kpe, pos2, freq, g_q_latent[l][None, :], g_kv_latent[l][None, :],
                              p["wq_h"], p["wkv_h"], p["gqh"], p["gkh"], padq, padk, batch, seq)
        ya = lax.cond(use_shift, functools.partial(_mla_attn, shifted=True),
                      functools.partial(_mla_attn, shifted=False), q, k, vt)
        yb = _sb_attn(proj, batch, seq)
        x1, h2 = _merge_out(ya, yb, proj, x2, ada3, g_norm2[l][None, :], p["wa"], p["wb"], p["wo"], seq)
        x2 = _ffn(h2, x1, ada3, p["w_ffn"], p["wf"], seq)
    return x2.reshape(batch, seq, d)
```

```python
import functools

import jax
import jax.numpy as jnp
import numpy as np
from jax import lax
from jax.experimental import pallas as pl
from jax.experimental.pallas import tpu as pltpu

D_MODEL = 2048
N_HEADS = 8
LATENT = 512
NOPE = 128
ROPE = 64
HALF = ROPE // 2
QK_DIM = NOPE + ROPE
QK_PAD = 256
V_DIM = 128
SB_DIM = 128
SUB = 128
MLA_HEADS_PER_STEP = 4
SB_HEADS_PER_STEP = 8
HEADS_W = N_HEADS * 128
ROPE_THETA = 10000.0
EPS = 1e-6
LANES = 128
NEG_BIG = -1e30
LOG2_E = 1.4426950408889634
SB_DEAD_LOG2 = -150.0
SHIFT_MAX = 60.0

COL_CQ, COL_CKV, COL_QSB, COL_KSB, COL_VSB, COL_GA, COL_GB = 0, 512, 1024, 2048, 3072, 4096, 6144
PROJ_W = 8192

VMEM_LIMIT = 56 * 1024 * 1024

f32 = jnp.float32
bf16 = jnp.bfloat16


def _cparams(sem):
    return pltpu.CompilerParams(dimension_semantics=sem, vmem_limit_bytes=VMEM_LIMIT)


def _dot(a, b):
    return jnp.dot(a, b, preferred_element_type=f32)


def _dot_nt(a, b):
    return lax.dot_general(a, b, (((1,), (1,)), ((), ())), preferred_element_type=f32)


def _rms_rows(xf, g_row):
    return xf * lax.rsqrt(jnp.mean(xf * xf, axis=-1, keepdims=True) + EPS) * g_row


def _ada_kernel(c_ref, w_ref, b_ref, o_ref):
    c = c_ref[...]
    c_act = (c * (1.0 / (1.0 + jnp.exp(-c)))).astype(bf16)
    o_ref[...] = _dot(c_act, w_ref[...].astype(bf16)) + b_ref[...]


def _ada(c_pad, w_ada, b_ada):
    rows, d = c_pad.shape
    n = w_ada.shape[1]
    tn = 1024
    return pl.pallas_call(
        _ada_kernel,
        out_shape=jax.ShapeDtypeStruct((rows, n), f32),
        grid=(n // tn,),
        in_specs=[pl.BlockSpec((rows, d), lambda j: (0, 0)),
                  pl.BlockSpec((d, tn), lambda j: (0, j)),
                  pl.BlockSpec((1, tn), lambda j: (0, j))],
        out_specs=pl.BlockSpec((rows, tn), lambda j: (0, j)),
        compiler_params=_cparams(("arbitrary",)),
        name="ada",
    )(c_pad, w_ada, b_ada)


def _in_proj_kernel(x_ref, ada_ref, g_ref, wl_ref, w_ref, wk_ref, proj_ref, kpe_ref, h_ref):
    n = pl.program_id(1)

    @pl.when(n == 0)
    def _():
        shift = ada_ref[0, 0:1, :]
        scale = ada_ref[0, 1:2, :]
        h = _rms_rows(x_ref[...], g_ref[...]) * (1.0 + scale) + shift
        h_ref[...] = h.astype(bf16)
        kpe_ref[...] = _dot(h_ref[...], wk_ref[...])
        proj_ref[...] = _dot(h_ref[...], wl_ref[...]).astype(bf16)

    @pl.when(n > 0)
    def _():
        proj_ref[...] = _dot(h_ref[...], w_ref[...]).astype(bf16)


def _in_proj(x2, ada3, g1, w_lat, w_rest, w_kpe, seq):
    t, d = x2.shape
    tm = 1024
    tn = w_lat.shape[1]
    per_b = seq // tm
    return pl.pallas_call(
        _in_proj_kernel,
        out_shape=(jax.ShapeDtypeStruct((t, PROJ_W), bf16), jax.ShapeDtypeStruct((t, 2 * LANES), f32)),
        grid=(t // tm, PROJ_W // tn),
        in_specs=[pl.BlockSpec((tm, d), lambda i, n: (i, 0)),
                  pl.BlockSpec((1, 6, d), lambda i, n: (i // per_b, 0, 0)),
                  pl.BlockSpec((1, d), lambda i, n: (0, 0)),
                  pl.BlockSpec((d, tn), lambda i, n: (0, 0)),
                  pl.BlockSpec((d, tn), lambda i, n: (0, jnp.maximum(n - 1, 0))),
                  pl.BlockSpec((d, 2 * LANES), lambda i, n: (0, 0))],
        out_specs=(pl.BlockSpec((tm, tn), lambda i, n: (i, n)),
                   pl.BlockSpec((tm, 2 * LANES), lambda i, n: (i, 0))),
        scratch_shapes=[pltpu.VMEM((tm, d), bf16)],
        compiler_params=_cparams(("arbitrary", "arbitrary")),
        name="in_proj",
    )(x2, ada3, g1, w_lat, w_rest, w_kpe)


def _row_sumsq(x):
    return _dot((x * x).astype(bf16), jnp.ones((x.shape[1], LANES), bf16))


def _latent_up_kernel(cq_ref, ckv_ref, kpe_ref, pos_ref, freq_ref, gql_ref, gkvl_ref, wq_ref, wkv_ref,
                      gqh_ref, gkh_ref, padq_ref, padk_ref, q_ref, k_ref, vt_ref,
                      cqn_ref, ckvn_ref, cos_ref, sin_ref, kss_ref):
    cqn_ref[...] = _rms_rows(cq_ref[...].astype(f32), gql_ref[...]).astype(bf16)
    ckvn_ref[...] = _rms_rows(ckv_ref[...].astype(f32), gkvl_ref[...]).astype(bf16)
    ang = pos_ref[...].astype(f32) * freq_ref[...]
    cos_ref[...] = jnp.cos(ang)
    sin_ref[...] = jnp.sin(ang)
    kss_ref[...] = _row_sumsq(kpe_ref[:, 0:LANES])
    gq = gqh_ref[...] * (LOG2_E * QK_DIM ** -0.5)
    gk = gkh_ref[...]

    def head(h, _):
        cos_t, sin_t = cos_ref[...], sin_ref[...]
        q = _dot(cqn_ref[...], wq_ref[h])
        q_r = lax.rsqrt(_row_sumsq(q[:, 0:QK_PAD]) * (1.0 / QK_DIM) + EPS)
        q_ref[0, h, :, 0:NOPE] = (q[:, 0:NOPE] * q_r * gq[:, 0:NOPE]).astype(bf16)
        q_ref[0, h, :, NOPE:QK_PAD] = (q[:, NOPE:QK_PAD] * q_r * gq[:, NOPE:QK_PAD] * cos_t
                                       + q[:, QK_PAD:] * q_r * gq[:, QK_PAD:] * sin_t + padq_ref[...]).astype(bf16)

        kv = _dot(ckvn_ref[...], wkv_ref[h])
        k_nope = kv[:, 0:NOPE]
        k_r = lax.rsqrt((_row_sumsq(k_nope) + kss_ref[...]) * (1.0 / QK_DIM) + EPS)
        k_ref[0, h, :, 0:NOPE] = (k_nope * k_r * gk[:, 0:NOPE]).astype(bf16)
        k_ref[0, h, :, NOPE:QK_PAD] = (kpe_ref[:, 0:LANES] * k_r * gk[:, NOPE:QK_PAD] * cos_t
                                       + kpe_ref[:, LANES:] * k_r * gk[:, QK_PAD:] * sin_t + padk_ref[...]).astype(bf16)
        vt_ref[0, h] = kv[:, NOPE:QK_PAD].T.astype(bf16)
        return 0

    lax.fori_loop(0, N_HEADS, head, 0, unroll=2)


def _latent_up(proj, kpe, pos2, freq, gql, gkvl, wq_h, wkv_h, gqh, gkh, padq, padk, batch, seq):
    t = proj.shape[0]
    tm = 512
    per_b = seq // tm
    q_w = wq_h.shape[2]
    heads_out = lambda w: pl.BlockSpec((1, N_HEADS, tm, w), lambda i: (i // per_b, 0, i % per_b, 0))
    row = lambda w: pl.BlockSpec((1, w), lambda i: (0, 0))
    return pl.pallas_call(
        _latent_up_kernel,
        out_shape=(jax.ShapeDtypeStruct((batch, N_HEADS, seq, QK_PAD), bf16),
                   jax.ShapeDtypeStruct((batch, N_HEADS, seq, QK_PAD), bf16),
                   jax.ShapeDtypeStruct((batch, N_HEADS, V_DIM, seq), bf16)),
        grid=(t // tm,),
        in_specs=[pl.BlockSpec((tm, LATENT), lambda i: (i, COL_CQ // LATENT)),
                  pl.BlockSpec((tm, LATENT), lambda i: (i, COL_CKV // LATENT)),
                  pl.BlockSpec((tm, 2 * LANES), lambda i: (i, 0)),
                  pl.BlockSpec((tm, 1), lambda i: (i, 0)),
                  row(LANES), row(LATENT), row(LATENT),
                  pl.BlockSpec((N_HEADS, LATENT, q_w), lambda i: (0, 0, 0)),
                  pl.BlockSpec((N_HEADS, LATENT, QK_PAD), lambda i: (0, 0, 0)),
                  row(q_w), row(q_w), row(LANES), row(LANES)],
        out_specs=(heads_out(QK_PAD), heads_out(QK_PAD),
                   pl.BlockSpec((1, N_HEADS, V_DIM, tm), lambda i: (i // per_b, 0, 0, i % per_b))),
        scratch_shapes=[pltpu.VMEM((tm, LATENT), bf16), pltpu.VMEM((tm, LATENT), bf16),
                        pltpu.VMEM((tm, LANES), f32), pltpu.VMEM((tm, LANES), f32), pltpu.VMEM((tm, LANES), f32)],
        compiler_params=_cparams(("arbitrary",)),
        name="latent_up",
    )(proj, proj, kpe, pos2, freq, gql, gkvl, wq_h, wkv_h, gqh, gkh, padq, padk)


def _mla_kernel(q_ref, k_ref, vt_ref, o_ref, acc_ref, *, tq, tk, g):
    qi = pl.program_id(2)

    def step(off, carry, masked):
        out = []
        scores = [_dot_nt(k_ref[0, hh, pl.ds(off, tk), :], q_ref[0, hh]) for hh in range(g)]
        for hh in range(g):
            m, l = carry[hh]
            s = scores[hh]
            if masked:
                key = lax.broadcasted_iota(jnp.int32, s.shape, 0)
                qry = lax.broadcasted_iota(jnp.int32, s.shape, 1)
                s = jnp.where(key <= qry, s, NEG_BIG)
            m_new = jnp.maximum(m, jnp.max(s, axis=0, keepdims=True))
            p = jnp.exp2(s - m_new)
            alpha = jnp.exp2(m - m_new)
            l_new = alpha * l + jnp.sum(p, axis=0, keepdims=True)
            acc_ref[hh] = alpha * acc_ref[hh] + _dot(vt_ref[0, hh, :, pl.ds(off, tk)], p.astype(bf16))
            out.append((m_new, l_new))
        return tuple(out)

    acc_ref[...] = jnp.zeros_like(acc_ref)
    init = tuple((jnp.full((1, tq), NEG_BIG, f32), jnp.zeros((1, tq), f32)) for _ in range(g))
    carry = lax.fori_loop(0, qi, lambda j, c: step(pl.multiple_of(j * tk, tk), c, False), init)
    carry = step(pl.multiple_of(qi * tk, tk), carry, True)
    for hh in range(g):
        o_ref[:, hh * V_DIM:(hh + 1) * V_DIM] = (acc_ref[hh] / carry[hh][1]).T.astype(bf16)


def _mla_shifted_kernel(q_ref, k_ref, vt_ref, o_ref, acc_ref, *, tq, tk, g):
    qi = pl.program_id(2)

    def step(off, ls, masked):
        out = []
        scores = [_dot_nt(k_ref[0, hh, pl.ds(off, tk), :], q_ref[0, hh]) for hh in range(g)]
        for hh in range(g):
            s = scores[hh]
            if masked:
                key = lax.broadcasted_iota(jnp.int32, s.shape, 0)
                qry = lax.broadcasted_iota(jnp.int32, s.shape, 1)
                s = jnp.where(key <= qry, s, NEG_BIG)
            p = jnp.exp2(s)
            acc_ref[hh] += _dot(vt_ref[0, hh, :, pl.ds(off, tk)], p.astype(bf16))
            out.append(ls[hh] + jnp.sum(p, axis=0, keepdims=True))
        return tuple(out)

    acc_ref[...] = jnp.zeros_like(acc_ref)
    init = tuple(jnp.zeros((1, tq), f32) for _ in range(g))
    ls = lax.fori_loop(0, qi, lambda j, c: step(pl.multiple_of(j * tk, tk), c, False), init)
    ls = step(pl.multiple_of(qi * tk, tk), ls, True)
    for hh in range(g):
        o_ref[:, hh * V_DIM:(hh + 1) * V_DIM] = (acc_ref[hh] / ls[hh]).T.astype(bf16)


def _mla_attn(q, k, vt, shifted):
    batch, heads, seq, _ = q.shape
    tq = tk = 512
    g = MLA_HEADS_PER_STEP
    nq = seq // tq
    body = _mla_shifted_kernel if shifted else _mla_kernel
    return pl.pallas_call(
        functools.partial(body, tq=tq, tk=tk, g=g),
        out_shape=jax.ShapeDtypeStruct((batch * seq, heads * V_DIM), bf16),
        grid=(batch, heads // g, nq),
        in_specs=[pl.BlockSpec((1, g, tq, QK_PAD), lambda b, h, i: (b, h, i, 0)),
                  pl.BlockSpec((1, g, seq, QK_PAD), lambda b, h, i: (b, h, 0, 0)),
                  pl.BlockSpec((1, g, V_DIM, seq), lambda b, h, i: (b, h, 0, 0))],
        out_specs=pl.BlockSpec((tq, g * V_DIM), lambda b, h, i: (b * nq + i, h)),
        scratch_shapes=[pltpu.VMEM((g, V_DIM, tq), f32)],
        compiler_params=_cparams(("arbitrary", "arbitrary", "arbitrary")),
        name="mla_attn_shifted" if shifted else "mla_attn",
    )(q, k, vt)


def _sb_kernel(q_ref, k_ref, v_ref, o_ref, acc_ref, *, tq, tk, g):
    qi = pl.program_id(2)
    n_sub = tk // SUB
    ss = lax.broadcasted_iota(jnp.int32, (SUB, SUB), 0)
    jj = lax.broadcasted_iota(jnp.int32, (SUB, SUB), 1)
    later = jnp.where(jj > ss, -1.0, 0.0).astype(bf16)
    later2 = jnp.concatenate([later, later], axis=1)
    head_cols = [slice(hh * SB_DIM, (hh + 1) * SB_DIM) for hh in range(g)]
    qs = [(q_ref[:, cols].astype(f32) * (LOG2_E * SB_DIM ** -0.5)).astype(bf16) for cols in head_cols]

    sub_rows = [slice(sb * SUB, (sb + 1) * SUB) for sb in range(n_sub)]

    def step(off, carry, masked):
        zs = [_dot_nt(k_ref[pl.ds(off, tk), head_cols[hh]], qs[hh]) for hh in range(g)]
        if masked:
            key = lax.broadcasted_iota(jnp.int32, (tk, tq), 0)
            qry = lax.broadcasted_iota(jnp.int32, (tk, tq), 1)
            keep = key < qry
        nloms, log_betas, tails = [], [], []
        for hh in range(g):
            z = zs[hh]
            nlom = jnp.maximum(z, 0.0) + jnp.log(1.0 + jnp.exp2(-jnp.abs(z))) * LOG2_E
            log_betas.append(z - nlom)
            if masked:
                nlom = jnp.where(keep, nlom, 0.0)
            nloms.append(nlom)
            head_tails = []
            for rows in sub_rows:
                nlom_sb = nlom[rows, :]
                hi = nlom_sb.astype(bf16)
                lo = (nlom_sb - hi.astype(f32)).astype(bf16)
                head_tails.append(_dot(later2, jnp.concatenate([hi, lo], axis=0)))
            tails.append(head_tails)
        out = []
        for hh in range(g):
            c = carry[hh]
            a_parts = [None] * n_sub
            for sb in reversed(range(n_sub)):
                rows = sub_rows[sb]
                tail = tails[hh][sb]
                a_sb = jnp.exp2(log_betas[hh][rows, :] + tail + c)
                if masked:
                    a_sb = jnp.where(keep[rows, :], a_sb, 0.0)
                a_parts[sb] = a_sb.astype(bf16)
                c = c + (tail[0:1, :] - nloms[hh][rows, :][0:1, :])
            a = jnp.concatenate(a_parts, axis=0)
            acc_ref[hh] += lax.dot_general(v_ref[pl.ds(off, tk), head_cols[hh]], a, (((0,), (0,)), ((), ())),
                                           preferred_element_type=f32)
            out.append(c)
        return tuple(out)

    def any_live(carry):
        c_max = carry[0]
        for c in carry[1:]:
            c_max = jnp.maximum(c_max, c)
        return jnp.max(c_max) > SB_DEAD_LOG2

    acc_ref[...] = jnp.zeros_like(acc_ref)
    carry = step(pl.multiple_of(qi * tk, tk), tuple(jnp.zeros((1, tq), f32) for _ in range(g)), True)

    def body(state):
        t, carry, _ = state
        carry = step(pl.multiple_of((qi - 1 - t) * tk, tk), carry, False)
        return t + 1, carry, any_live(carry)

    lax.while_loop(lambda s: (s[0] < qi) & s[2], body, (jnp.int32(0), carry, any_live(carry)))
    for hh in range(g):
        o_ref[:, hh * SB_DIM:(hh + 1) * SB_DIM] = acc_ref[hh].T.astype(bf16)


def _sb_attn(proj, batch, seq):
    tq = tk = 256
    g = SB_HEADS_PER_STEP
    w = g * SB_DIM
    nq = seq // tq
    q0, k0, v0 = COL_QSB // w, COL_KSB // w, COL_VSB // w
    return pl.pallas_call(
        functools.partial(_sb_kernel, tq=tq, tk=tk, g=g),
        out_shape=jax.ShapeDtypeStruct((batch * seq, HEADS_W), bf16),
        grid=(batch, N_HEADS // g, nq),
        in_specs=[pl.BlockSpec((tq, w), lambda b, h, i: (b * nq + i, q0 + h)),
                  pl.BlockSpec((seq, w), lambda b, h, i: (b, k0 + h)),
                  pl.BlockSpec((seq, w), lambda b, h, i: (b, v0 + h))],
        out_specs=pl.BlockSpec((tq, w), lambda b, h, i: (b * nq + i, h)),
        scratch_shapes=[pltpu.VMEM((g, SB_DIM, tq), f32)],
        compiler_params=_cparams(("arbitrary", "arbitrary", "arbitrary")),
        name="sb_attn",
    )(proj, proj, proj)


def _merge_out_kernel(ya_ref, yb_ref, ga_ref, gb_ref, x_ref, ada_ref, g2_ref, wa_ref, wb_ref, wo_ref,
                      x1_ref, h2_ref):
    sig = lambda ref: 1.0 / (1.0 + jnp.exp(-ref[...].astype(f32)))
    merged = sig(ga_ref) * _dot(ya_ref[...], wa_ref[...]) + sig(gb_ref) * _dot(yb_ref[...], wb_ref[...])
    x1 = x_ref[...] + ada_ref[0, 2:3, :] * _dot(merged.astype(bf16), wo_ref[...])
    x1_ref[...] = x1
    h2 = _rms_rows(x1, g2_ref[...]) * (1.0 + ada_ref[0, 4:5, :]) + ada_ref[0, 3:4, :]
    h2_ref[...] = h2.astype(bf16)


def _merge_out(ya, yb, proj, x2, ada3, g2, wa, wb, wo, seq):
    t, d = x2.shape
    tm = 256
    per_b = seq // tm
    const = lambda shape: pl.BlockSpec(shape, lambda i: (0, 0), pipeline_mode=pl.Buffered(1))
    return pl.pallas_call(
        _merge_out_kernel,
        out_shape=(jax.ShapeDtypeStruct((t, d), f32), jax.ShapeDtypeStruct((t, d), bf16)),
        grid=(t // tm,),
        in_specs=[pl.BlockSpec((tm, HEADS_W), lambda i: (i, 0)),
                  pl.BlockSpec((tm, HEADS_W), lambda i: (i, 0)),
                  pl.BlockSpec((tm, d), lambda i: (i, COL_GA // d)),
                  pl.BlockSpec((tm, d), lambda i: (i, COL_GB // d)),
                  pl.BlockSpec((tm, d), lambda i: (i, 0)),
                  pl.BlockSpec((1, 6, d), lambda i: (i // per_b, 0, 0)),
                  pl.BlockSpec((1, d), lambda i: (0, 0)),
                  const((HEADS_W, d)), const((HEADS_W, d)), const((d, d))],
        out_specs=(pl.BlockSpec((tm, d), lambda i: (i, 0)), pl.BlockSpec((tm, d), lambda i: (i, 0))),
        compiler_params=_cparams(("arbitrary",)),
        name="merge_out",
    )(ya, yb, proj, proj, x2, ada3, g2, wa, wb, wo)


def _ffn_kernel(h_ref, x1_ref, ada_ref, wg_ref, wu_ref, wo_ref, o_ref, acc_ref, act_ref, *, n_f):
    s = pl.program_id(0)
    f_prev = lax.rem(s + (n_f - 1), n_f)

    @pl.when(s == 0)
    def _():
        acc_ref[...] = jnp.zeros_like(acc_ref)
        act_ref[...] = jnp.zeros_like(act_ref)

    h = h_ref[...]
    gate = _dot(h, wg_ref[...])
    up = _dot(h, wu_ref[...])
    acc_ref[...] += _dot(act_ref[...], wo_ref[...])
    act_ref[...] = (gate * (1.0 / (1.0 + jnp.exp(-gate))) * up).astype(bf16)

    @pl.when((f_prev == n_f - 1) & (s > 0))
    def _():
        o_ref[...] = x1_ref[...] + ada_ref[0, 5:6, :] * acc_ref[...]
        acc_ref[...] = jnp.zeros_like(acc_ref)


def _ffn(h2, x1, ada3, w_in, wo, seq):
    t, d = x1.shape
    d_ff = wo.shape[0]
    tm, tf = 512, 512
    per_b = seq // tm
    n_f = d_ff // tf
    n_m = t // tm
    cur_m = lambda s: jnp.minimum(s // n_f, n_m - 1)
    prev_m = lambda s: jnp.maximum(s - 1, 0) // n_f
    prev_f = lambda s: lax.rem(jnp.maximum(s - 1, 0), n_f)
    return pl.pallas_call(
        functools.partial(_ffn_kernel, n_f=n_f),
        out_shape=jax.ShapeDtypeStruct((t, d), f32),
        grid=(n_m * n_f + 1,),
        in_specs=[pl.BlockSpec((tm, d), lambda s: (cur_m(s), 0)),
                  pl.BlockSpec((tm, d), lambda s: (prev_m(s), 0)),
                  pl.BlockSpec((1, 6, d), lambda s: (prev_m(s) // per_b, 0, 0)),
                  pl.BlockSpec((d, tf), lambda s: (0, lax.rem(s, n_f))),
                  pl.BlockSpec((d, tf), lambda s: (0, n_f + lax.rem(s, n_f))),
                  pl.BlockSpec((tf, d), lambda s: (prev_f(s), 0))],
        out_specs=pl.BlockSpec((tm, d), lambda s: (prev_m(s), 0)),
        scratch_shapes=[pltpu.VMEM((tm, d), f32), pltpu.VMEM((tm, tf), bf16)],
        compiler_params=_cparams(("arbitrary",)),
        name="ffn",
    )(h2, x1, ada3, w_in, w_in, wo)


def _score_shift(g_q_head, g_k_head):
    c = LOG2_E * QK_DIM ** -0.5
    return 1.02 * QK_DIM * c * jnp.max(jnp.abs(g_q_head)) * jnp.max(jnp.abs(g_k_head))


def _prep_layer(w_in, w_uq, w_ukv, g_q_head, g_k_head, w_proj_mla, w_proj_sb, w_out, w_ffn_in, w_ffn_out):
    o_kpe, o_qsb = 2 * LATENT, 2 * LATENT + ROPE

    def with_swapped(w, axis):
        x1, x2 = jnp.split(w, 2, axis=axis)
        shape = list(w.shape)
        shape[axis] = LANES - ROPE
        z = jnp.zeros(shape, w.dtype)
        return jnp.concatenate([x1, x2, z, x2, x1, z], axis=axis)

    w_lat = w_in[:, :o_kpe].astype(bf16)
    w_rest = w_in[:, o_qsb:].astype(bf16)
    w_kpe = with_swapped(w_in[:, o_kpe:o_qsb], 1).astype(bf16)
    wq = w_uq.reshape(LATENT, N_HEADS, QK_DIM).transpose(1, 0, 2)
    wq_h = jnp.concatenate([wq[..., :NOPE], with_swapped(wq[..., NOPE:], 2)], axis=2).astype(bf16)
    wkv_h = w_ukv.reshape(LATENT, N_HEADS, NOPE + V_DIM).transpose(1, 0, 2).astype(bf16)

    def head_gain(g):
        g1, g2 = g[NOPE:NOPE + HALF], g[NOPE + HALF:]
        z = jnp.zeros((LANES - ROPE,), g.dtype)
        return jnp.concatenate([g[:NOPE], g1, g2, z, -g2, g1, z])[None, :]

    return dict(w_lat=w_lat, w_rest=w_rest, w_kpe=w_kpe, wq_h=wq_h, wkv_h=wkv_h,
                gqh=head_gain(g_q_head), gkh=head_gain(g_k_head),
                wa=w_proj_mla.astype(bf16), wb=w_proj_sb.astype(bf16), wo=w_out.astype(bf16),
                w_ffn=w_ffn_in.astype(bf16), wf=w_ffn_out.astype(bf16))


def kernel(x, c, positions, w_ada, b_ada, g_norm1, g_norm2, w_in, g_q_latent, g_kv_latent, w_uq, w_ukv,
           g_q_head, g_k_head, w_proj_mla, w_proj_sb, w_out, w_ffn_in, w_ffn_out):
    batch, seq, d = x.shape
    depth = w_ada.shape[0]
    t = batch * seq
    x2 = x.reshape(t, d)
    pos2 = positions.reshape(t, 1)
    c_pad = jnp.pad(c, ((0, 8 - batch), (0, 0)))
    lane = np.arange(LANES)
    freq = jnp.where(lane < ROPE, ROPE_THETA ** (-jnp.asarray(lane % HALF, f32) / HALF), 0.0)[None, :]

    for l in range(depth):
        p = _prep_layer(w_in[l], w_uq[l], w_ukv[l], g_q_head[l], g_k_head[l], w_proj_mla[l], w_proj_sb[l],
                        w_out[l], w_ffn_in[l], w_ffn_out[l])
        ada = _ada(c_pad, w_ada[l], b_ada[l][None, :])
        ada3 = ada[:batch].reshape(batch, 6, d)
        proj, kpe = _in_proj(x2, ada3, g_norm1[l][None, :], p["w_lat"], p["w_rest"], p["w_kpe"], seq)
        shift = _score_shift(g_q_head[l], g_k_head[l])
        use_shift = shift <= SHIFT_MAX
        pad_lane = (lane == ROPE).astype(f32)[None, :]
        padq = pad_lane * use_shift.astype(f32)
        padk = pad_lane * jnp.where(use_shift, -shift, 0.0)
        q, k, vt = _latent_up(proj, kpe, pos2, freq, g_q_latent[l][None, :], g_kv_latent[l][None, :],
                              p["wq_h"], p["wkv_h"], p["gqh"], p["gkh"], padq, padk, batch, seq)
        ya = lax.cond(use_shift, functools.partial(_mla_attn, shifted=True),
                      functools.partial(_mla_attn, shifted=False), q, k, vt)
        yb = _sb_attn(proj, batch, seq)
        x1, h2 = _merge_out(ya, yb, proj, x2, ada3, g_norm2[l][None, :], p["wa"], p["wb"], p["wo"], seq)
        x2 = _ffn(h2, x1, ada3, p["w_ffn"], p["wf"], seq)
    return x2.reshape(batch, seq, d)
```

```python
import functools

import jax
import jax.numpy as jnp
import numpy as np
from jax import lax
from jax.experimental import pallas as pl
from jax.experimental.pallas import tpu as pltpu

D_MODEL = 2048
N_HEADS = 8
LATENT = 512
NOPE = 128
ROPE = 64
HALF = ROPE // 2
QK_DIM = NOPE + ROPE
QK_PAD = 256
V_DIM = 128
SB_DIM = 128
SUB = 128
MLA_HEADS_PER_STEP = 4
SB_HEADS_PER_STEP = 8
HEADS_W = N_HEADS * 128
ROPE_THETA = 10000.0
EPS = 1e-6
LANES = 128
NEG_BIG = -1e30
LOG2_E = 1.4426950408889634
SB_DEAD_LOG2 = -150.0
SHIFT_MAX = 60.0

COL_CQ, COL_CKV, COL_QSB, COL_KSB, COL_VSB, COL_GA, COL_GB = 0, 512, 1024, 2048, 3072, 4096, 6144
PROJ_W = 8192

VMEM_LIMIT = 56 * 1024 * 1024

f32 = jnp.float32
bf16 = jnp.bfloat16


def _cparams(sem):
    return pltpu.CompilerParams(dimension_semantics=sem, vmem_limit_bytes=VMEM_LIMIT)


def _dot(a, b):
    return jnp.dot(a, b, preferred_element_type=f32)


def _dot_nt(a, b):
    return lax.dot_general(a, b, (((1,), (1,)), ((), ())), preferred_element_type=f32)


def _rms_rows(xf, g_row):
    return xf * lax.rsqrt(jnp.mean(xf * xf, axis=-1, keepdims=True) + EPS) * g_row


def _ada_kernel(c_ref, w_ref, b_ref, o_ref):
    c = c_ref[...]
    c_act = (c * (1.0 / (1.0 + jnp.exp(-c)))).astype(bf16)
    o_ref[...] = _dot(c_act, w_ref[...].astype(bf16)) + b_ref[...]


def _ada(c_pad, w_ada, b_ada):
    rows, d = c_pad.shape
    n = w_ada.shape[1]
    tn = 1024
    return pl.pallas_call(
        _ada_kernel,
        out_shape=jax.ShapeDtypeStruct((rows, n), f32),
        grid=(n // tn,),
        in_specs=[pl.BlockSpec((rows, d), lambda j: (0, 0)),
                  pl.BlockSpec((d, tn), lambda j: (0, j)),
                  pl.BlockSpec((1, tn), lambda j: (0, j))],
        out_specs=pl.BlockSpec((rows, tn), lambda j: (0, j)),
        compiler_params=_cparams(("arbitrary",)),
        name="ada",
    )(c_pad, w_ada, b_ada)


def _in_proj_kernel(x_ref, ada_ref, g_ref, wl_ref, w_ref, wk_ref, proj_ref, kpe_ref, h_ref):
    n = pl.program_id(1)

    @pl.when(n == 0)
    def _():
        shift = ada_ref[0, 0:1, :]
        scale = ada_ref[0, 1:2, :]
        h = _rms_rows(x_ref[...], g_ref[...]) * (1.0 + scale) + shift
        h_ref[...] = h.astype(bf16)
        kpe_ref[...] = _dot(h_ref[...], wk_ref[...])
        proj_ref[...] = _dot(h_ref[...], wl_ref[...]).astype(bf16)

    @pl.when(n > 0)
    def _():
        proj_ref[...] = _dot(h_ref[...], w_ref[...]).astype(bf16)


def _in_proj(x2, ada3, g1, w_lat, w_rest, w_kpe, seq):
    t, d = x2.shape
    tm = 1024
    tn = w_lat.shape[1]
    per_b = seq // tm
    return pl.pallas_call(
        _in_proj_kernel,
        out_shape=(jax.ShapeDtypeStruct((t, PROJ_W), bf16), jax.ShapeDtypeStruct((t, 2 * LANES), f32)),
        grid=(t // tm, PROJ_W // tn),
        in_specs=[pl.BlockSpec((tm, d), lambda i, n: (i, 0)),
                  pl.BlockSpec((1, 6, d), lambda i, n: (i // per_b, 0, 0)),
                  pl.BlockSpec((1, d), lambda i, n: (0, 0)),
                  pl.BlockSpec((d, tn), lambda i, n: (0, 0)),
                  pl.BlockSpec((d, tn), lambda i, n: (0, jnp.maximum(n - 1, 0))),
                  pl.BlockSpec((d, 2 * LANES), lambda i, n: (0, 0))],
        out_specs=(pl.BlockSpec((tm, tn), lambda i, n: (i, n)),
                   pl.BlockSpec((tm, 2 * LANES), lambda i, n: (i, 0))),
        scratch_shapes=[pltpu.VMEM((tm, d), bf16)],
        compiler_params=_cparams(("arbitrary", "arbitrary")),
        name="in_proj",
    )(x2, ada3, g1, w_lat, w_rest, w_kpe)


def _row_sumsq(x):
    return _dot((x * x).astype(bf16), jnp.ones((x.shape[1], LANES), bf16))


def _latent_up_kernel(cq_ref, ckv_ref, kpe_ref, pos_ref, freq_ref, gql_ref, gkvl_ref, wq_ref, wkv_ref,
                      gqh_ref, gkh_ref, padq_ref, padk_ref, q_ref, k_ref, vt_ref,
                      cqn_ref, ckvn_ref, cos_ref, sin_ref, kss_ref):
    cqn_ref[...] = _rms_rows(cq_ref[...].astype(f32), gql_ref[...]).astype(bf16)
    ckvn_ref[...] = _rms_rows(ckv_ref[...].astype(f32), gkvl_ref[...]).astype(bf16)
    ang = pos_ref[...].astype(f32) * freq_ref[...]
    cos_ref[...] = jnp.cos(ang)
    sin_ref[...] = jnp.sin(ang)
    kss_ref[...] = _row_sumsq(kpe_ref[:, 0:LANES])
    gq = gqh_ref[...] * (LOG2_E * QK_DIM ** -0.5)
    gk = gkh_ref[...]

    def head(h, _):
        cos_t, sin_t = cos_ref[...], sin_ref[...]
        q = _dot(cqn_ref[...], wq_ref[h])
        q_r = lax.rsqrt(_row_sumsq(q[:, 0:QK_PAD]) * (1.0 / QK_DIM) + EPS)
        q_ref[0, h, :, 0:NOPE] = (q[:, 0:NOPE] * q_r * gq[:, 0:NOPE]).astype(bf16)
        q_ref[0, h, :, NOPE:QK_PAD] = (q[:, NOPE:QK_PAD] * q_r * gq[:, NOPE:QK_PAD] * cos_t
                                       + q[:, QK_PAD:] * q_r * gq[:, QK_PAD:] * sin_t + padq_ref[...]).astype(bf16)

        kv = _dot(ckvn_ref[...], wkv_ref[h])
        k_nope = kv[:, 0:NOPE]
        k_r = lax.rsqrt((_row_sumsq(k_nope) + kss_ref[...]) * (1.0 / QK_DIM) + EPS)
        k_ref[0, h, :, 0:NOPE] = (k_nope * k_r * gk[:, 0:NOPE]).astype(bf16)
        k_ref[0, h, :, NOPE:QK_PAD] = (kpe_ref[:, 0:LANES] * k_r * gk[:, NOPE:QK_PAD] * cos_t
                                       + kpe_ref[:, LANES:] * k_r * gk[:, QK_PAD:] * sin_t + padk_ref[...]).astype(bf16)
        vt_ref[0, h] = kv[:, NOPE:QK_PAD].T.astype(bf16)
        return 0

    lax.fori_loop(0, N_HEADS, head, 0, unroll=2)


def _latent_up(proj, kpe, pos2, freq, gql, gkvl, wq_h, wkv_h, gqh, gkh, padq, padk, batch, seq):
    t = proj.shape[0]
    tm = 512
    per_b = seq // tm
    q_w = wq_h.shape[2]
    heads_out = lambda w: pl.BlockSpec((1, N_HEADS, tm, w), lambda i: (i // per_b, 0, i % per_b, 0))
    row = lambda w: pl.BlockSpec((1, w), lambda i: (0, 0))
    return pl.pallas_call(
        _latent_up_kernel,
        out_shape=(jax.ShapeDtypeStruct((batch, N_HEADS, seq, QK_PAD), bf16),
                   jax.ShapeDtypeStruct((batch, N_HEADS, seq, QK_PAD), bf16),
                   jax.ShapeDtypeStruct((batch, N_HEADS, V_DIM, seq), bf16)),
        grid=(t // tm,),
        in_specs=[pl.BlockSpec((tm, LATENT), lambda i: (i, COL_CQ // LATENT)),
                  pl.BlockSpec((tm, LATENT), lambda i: (i, COL_CKV // LATENT)),
                  pl.BlockSpec((tm, 2 * LANES), lambda i: (i, 0)),
                  pl.BlockSpec((tm, 1), lambda i: (i, 0)),
                  row(LANES), row(LATENT), row(LATENT),
                  pl.BlockSpec((N_HEADS, LATENT, q_w), lambda i: (0, 0, 0)),
                  pl.BlockSpec((N_HEADS, LATENT, QK_PAD), lambda i: (0, 0, 0)),
                  row(q_w), row(q_w), row(LANES), row(LANES)],
        out_specs=(heads_out(QK_PAD), heads_out(QK_PAD),
                   pl.BlockSpec((1, N_HEADS, V_DIM, tm), lambda i: (i // per_b, 0, 0, i % per_b))),
        scratch_shapes=[pltpu.VMEM((tm, LATENT), bf16), pltpu.VMEM((tm, LATENT), bf16),
                        pltpu.VMEM((tm, LANES), f32), pltpu.VMEM((tm, LANES), f32), pltpu.VMEM((tm, LANES), f32)],
        compiler_params=_cparams(("arbitrary",)),
        name="latent_up",
    )(proj, proj, kpe, pos2, freq, gql, gkvl, wq_h, wkv_h, gqh, gkh, padq, padk)


def _mla_kernel(q_ref, k_ref, vt_ref, o_ref, acc_ref, *, tq, tk, g):
    qi = pl.program_id(2)

    def step(off, carry, masked):
        out = []
        scores = [_dot_nt(k_ref[0, hh, pl.ds(off, tk), :], q_ref[0, hh]) for hh in range(g)]
        for hh in range(g):
            m, l = carry[hh]
            s = scores[hh]
            if masked:
                key = lax.broadcasted_iota(jnp.int32, s.shape, 0)
                qry = lax.broadcasted_iota(jnp.int32, s.shape, 1)
                s = jnp.where(key <= qry, s, NEG_BIG)
            m_new = jnp.maximum(m, jnp.max(s, axis=0, keepdims=True))
            p = jnp.exp2(s - m_new)
            alpha = jnp.exp2(m - m_new)
            l_new = alpha * l + jnp.sum(p, axis=0, keepdims=True)
            acc_ref[hh] = alpha * acc_ref[hh] + _dot(vt_ref[0, hh, :, pl.ds(off, tk)], p.astype(bf16))
            out.append((m_new, l_new))
        return tuple(out)

    acc_ref[...] = jnp.zeros_like(acc_ref)
    init = tuple((jnp.full((1, tq), NEG_BIG, f32), jnp.zeros((1, tq), f32)) for _ in range(g))
    carry = lax.fori_loop(0, qi, lambda j, c: step(pl.multiple_of(j * tk, tk), c, False), init)
    carry = step(pl.multiple_of(qi * tk, tk), carry, True)
    for hh in range(g):
        o_ref[:, hh * V_DIM:(hh + 1) * V_DIM] = (acc_ref[hh] / carry[hh][1]).T.astype(bf16)


def _mla_shifted_kernel(q_ref, k_ref, vt_ref, o_ref, acc_ref, *, tq, tk, g):
    qi = pl.program_id(2)

    def step(off, ls, masked):
        out = []
        scores = [_dot_nt(k_ref[0, hh, pl.ds(off, tk), :], q_ref[0, hh]) for hh in range(g)]
        for hh in range(g):
            s = scores[hh]
            if masked:
                key = lax.broadcasted_iota(jnp.int32, s.shape, 0)
                qry = lax.broadcasted_iota(jnp.int32, s.shape, 1)
                s = jnp.where(key <= qry, s, NEG_BIG)
            p = jnp.exp2(s)
            acc_ref[hh] += _dot(vt_ref[0, hh, :, pl.ds(off, tk)], p.astype(bf16))
            out.append(ls[hh] + jnp.sum(p, axis=0, keepdims=True))
        return tuple(out)

    acc_ref[...] = jnp.zeros_like(acc_ref)
    init = tuple(jnp.zeros((1, tq), f32) for _ in range(g))
    ls = lax.fori_loop(0, qi, lambda j, c: step(pl.multiple_of(j * tk, tk), c, False), init)
    ls = step(pl.multiple_of(qi * tk, tk), ls, True)
    for hh in range(g):
        o_ref[:, hh * V_DIM:(hh + 1) * V_DIM] = (acc_ref[hh] / ls[hh]).T.astype(bf16)


def _mla_attn(q, k, vt, shifted):
    batch, heads, seq, _ = q.shape
    tq = tk = 512
    g = MLA_HEADS_PER_STEP
    nq = seq // tq
    body = _mla_shifted_kernel if shifted else _mla_kernel
    return pl.pallas_call(
        functools.partial(body, tq=tq, tk=tk, g=g),
        out_shape=jax.ShapeDtypeStruct((batch * seq, heads * V_DIM), bf16),
        grid=(batch, heads // g, nq),
        in_specs=[pl.BlockSpec((1, g, tq, QK_PAD), lambda b, h, i: (b, h, i, 0)),
                  pl.BlockSpec((1, g, seq, QK_PAD), lambda b, h, i: (b, h, 0, 0)),
                  pl.BlockSpec((1, g, V_DIM, seq), lambda b, h, i: (b, h, 0, 0))],
        out_specs=pl.BlockSpec((tq, g * V_DIM), lambda b, h, i: (b * nq + i, h)),
        scratch_shapes=[pltpu.VMEM((g, V_DIM, tq), f32)],
        compiler_params=_cparams(("arbitrary", "arbitrary", "arbitrary")),
        name="mla_attn_shifted" if shifted else "mla_attn",
    )(q, k, vt)


def _sb_kernel(q_ref, k_ref, v_ref, o_ref, acc_ref, *, tq, tk, g):
    qi = pl.program_id(2)
    n_sub = tk // SUB
    ss = lax.broadcasted_iota(jnp.int32, (SUB, SUB), 0)
    jj = lax.broadcasted_iota(jnp.int32, (SUB, SUB), 1)
    later = jnp.where(jj > ss, -1.0, 0.0).astype(bf16)
    later2 = jnp.concatenate([later, later], axis=1)
    head_cols = [slice(hh * SB_DIM, (hh + 1) * SB_DIM) for hh in range(g)]
    qs = [(q_ref[:, cols].astype(f32) * (LOG2_E * SB_DIM ** -0.5)).astype(bf16) for cols in head_cols]

    sub_rows = [slice(sb * SUB, (sb + 1) * SUB) for sb in range(n_sub)]

    def step(off, carry, masked):
        zs = [_dot_nt(k_ref[pl.ds(off, tk), head_cols[hh]], qs[hh]) for hh in range(g)]
        if masked:
            key = lax.broadcasted_iota(jnp.int32, (tk, tq), 0)
            qry = lax.broadcasted_iota(jnp.int32, (tk, tq), 1)
            keep = key < qry
        nloms, log_betas, tails = [], [], []
        for hh in range(g):
            z = zs[hh]
            nlom = jnp.maximum(z, 0.0) + jnp.log(1.0 + jnp.exp2(-jnp.abs(z))) * LOG2_E
            log_betas.append(z - nlom)
            if masked:
                nlom = jnp.where(keep, nlom, 0.0)
            nloms.append(nlom)
            head_tails = []
            for rows in sub_rows:
                nlom_sb = nlom[rows, :]
                hi = nlom_sb.astype(bf16)
                lo = (nlom_sb - hi.astype(f32)).astype(bf16)
                head_tails.append(_dot(later2, jnp.concatenate([hi, lo], axis=0)))
            tails.append(head_tails)
        out = []
        for hh in range(g):
            c = carry[hh]
            a_parts = [None] * n_sub
            for sb in reversed(range(n_sub)):
                rows = sub_rows[sb]
                tail = tails[hh][sb]
                a_sb = jnp.exp2(log_betas[hh][rows, :] + tail + c)
                if masked:
                    a_sb = jnp.where(keep[rows, :], a_sb, 0.0)
                a_parts[sb] = a_sb.astype(bf16)
                c = c + (tail[0:1, :] - nloms[hh][rows, :][0:1, :])
            a = jnp.concatenate(a_parts, axis=0)
            acc_ref[hh] += lax.dot_general(v_ref[pl.ds(off, tk), head_cols[hh]], a, (((0,), (0,)), ((), ())),
                                           preferred_element_type=f32)
            out.append(c)
        return tuple(out)

    def any_live(carry):
        c_max = carry[0]
        for c in carry[1:]:
            c_max = jnp.maximum(c_max, c)
        return jnp.max(c_max) > SB_DEAD_LOG2

    acc_ref[...] = jnp.zeros_like(acc_ref)
    carry = step(pl.multiple_of(qi * tk, tk), tuple(jnp.zeros((1, tq), f32) for _ in range(g)), True)

    def body(state):
        t, carry, _ = state
        carry = step(pl.multiple_of((qi - 1 - t) * tk, tk), carry, False)
        return t + 1, carry, any_live(carry)

    lax.while_loop(lambda s: (s[0] < qi) & s[2], body, (jnp.int32(0), carry, any_live(carry)))
    for hh in range(g):
        o_ref[:, hh * SB_DIM:(hh + 1) * SB_DIM] = acc_ref[hh].T.astype(bf16)


def _sb_attn(proj, batch, seq):
    tq = tk = 256
    g = SB_HEADS_PER_STEP
    w = g * SB_DIM
    nq = seq // tq
    q0, k0, v0 = COL_QSB // w, COL_KSB // w, COL_VSB // w
    return pl.pallas_call(
        functools.partial(_sb_kernel, tq=tq, tk=tk, g=g),
        out_shape=jax.ShapeDtypeStruct((batch * seq, HEADS_W), bf16),
        grid=(batch, N_HEADS // g, nq),
        in_specs=[pl.BlockSpec((tq, w), lambda b, h, i: (b * nq + i, q0 + h)),
                  pl.BlockSpec((seq, w), lambda b, h, i: (b, k0 + h)),
                  pl.BlockSpec((seq, w), lambda b, h, i: (b, v0 + h))],
        out_specs=pl.BlockSpec((tq, w), lambda b, h, i: (b * nq + i, h)),
        scratch_shapes=[pltpu.VMEM((g, SB_DIM, tq), f32)],
        compiler_params=_cparams(("arbitrary", "arbitrary", "arbitrary")),
        name="sb_attn",
    )(proj, proj, proj)


def _merge_out_kernel(ya_ref, yb_ref, ga_ref, gb_ref, x_ref, ada_ref, wa_ref, wb_ref, wo_ref, x1_ref):
    sig = lambda ref: 1.0 / (1.0 + jnp.exp(-ref[...].astype(f32)))
    merged = sig(ga_ref) * _dot(ya_ref[...], wa_ref[...]) + sig(gb_ref) * _dot(yb_ref[...], wb_ref[...])
    x1_ref[...] = x_ref[...] + ada_ref[0, 2:3, :] * _dot(merged.astype(bf16), wo_ref[...])


def _merge_out(ya, yb, proj, x2, ada3, wa, wb, wo, seq):
    t, d = x2.shape
    tm = 256
    per_b = seq // tm
    const = lambda shape: pl.BlockSpec(shape, lambda i: (0, 0), pipeline_mode=pl.Buffered(1))
    return pl.pallas_call(
        _merge_out_kernel,
        out_shape=jax.ShapeDtypeStruct((t, d), f32),
        grid=(t // tm,),
        in_specs=[pl.BlockSpec((tm, HEADS_W), lambda i: (i, 0)),
                  pl.BlockSpec((tm, HEADS_W), lambda i: (i, 0)),
                  pl.BlockSpec((tm, d), lambda i: (i, COL_GA // d)),
                  pl.BlockSpec((tm, d), lambda i: (i, COL_GB // d)),
                  pl.BlockSpec((tm, d), lambda i: (i, 0)),
                  pl.BlockSpec((1, 6, d), lambda i: (i // per_b, 0, 0)),
                  const((HEADS_W, d)), const((HEADS_W, d)), const((d, d))],
        out_specs=pl.BlockSpec((tm, d), lambda i: (i, 0)),
        compiler_params=_cparams(("arbitrary",)),
        name="merge_out",
    )(ya, yb, proj, proj, x2, ada3, wa, wb, wo)


def _ffn_kernel(x1_ref, ada_ref, g2_ref, wg_ref, wu_ref, wo_ref, o_ref, h_ref):
    f = pl.program_id(1)

    @pl.when(f == 0)
    def _():
        x1 = x1_ref[...]
        d = x1.shape[1]
        inv = lax.rsqrt(_row_sumsq(x1) * (1.0 / d) + EPS)
        inv = jnp.concatenate([inv] * (d // LANES), axis=1)
        h = x1 * inv * (g2_ref[...] * (1.0 + ada_ref[0, 4:5, :])) + ada_ref[0, 3:4, :]
        h_ref[...] = h.astype(bf16)
        o_ref[...] = jnp.zeros_like(o_ref)

    h = h_ref[...]
    gate = _dot(h, wg_ref[...])
    up = _dot(h, wu_ref[...])
    act = gate * (1.0 / (1.0 + jnp.exp(-gate))) * up
    o_ref[...] += _dot(act.astype(bf16), wo_ref[...])

    @pl.when(f == pl.num_programs(1) - 1)
    def _():
        o_ref[...] = x1_ref[...] + ada_ref[0, 5:6, :] * o_ref[...]


def _ffn(x1, ada3, g2, w_in, wo, seq):
    t, d = x1.shape
    d_ff = wo.shape[0]
    tm, tf = 1024, 512
    per_b = seq // tm
    n_f = d_ff // tf
    return pl.pallas_call(
        _ffn_kernel,
        out_shape=jax.ShapeDtypeStruct((t, d), f32),
        grid=(t // tm, n_f),
        in_specs=[pl.BlockSpec((tm, d), lambda i, f: (i, 0)),
                  pl.BlockSpec((1, 6, d), lambda i, f: (i // per_b, 0, 0)),
                  pl.BlockSpec((1, d), lambda i, f: (0, 0)),
                  pl.BlockSpec((d, tf), lambda i, f: (0, f)),
                  pl.BlockSpec((d, tf), lambda i, f: (0, n_f + f)),
                  pl.BlockSpec((tf, d), lambda i, f: (f, 0))],
        out_specs=pl.BlockSpec((tm, d), lambda i, f: (i, 0)),
        scratch_shapes=[pltpu.VMEM((tm, d), bf16)],
        compiler_params=_cparams(("arbitrary", "arbitrary")),
        name="ffn",
    )(x1, ada3, g2, w_in, w_in, wo)


def _score_shift(g_q_head, g_k_head):
    c = LOG2_E * QK_DIM ** -0.5
    return 1.02 * QK_DIM * c * jnp.max(jnp.abs(g_q_head)) * jnp.max(jnp.abs(g_k_head))


def _prep_layer(w_in, w_uq, w_ukv, g_q_head, g_k_head, w_proj_mla, w_proj_sb, w_out, w_ffn_in, w_ffn_out):
    o_kpe, o_qsb = 2 * LATENT, 2 * LATENT + ROPE

    def with_swapped(w, axis):
        x1, x2 = jnp.split(w, 2, axis=axis)
        shape = list(w.shape)
        shape[axis] = LANES - ROPE
        z = jnp.zeros(shape, w.dtype)
        return jnp.concatenate([x1, x2, z, x2, x1, z], axis=axis)

    w_lat = w_in[:, :o_kpe].astype(bf16)
    w_rest = w_in[:, o_qsb:].astype(bf16)
    w_kpe = with_swapped(w_in[:, o_kpe:o_qsb], 1).astype(bf16)
    wq = w_uq.reshape(LATENT, N_HEADS, QK_DIM).transpose(1, 0, 2)
    wq_h = jnp.concatenate([wq[..., :NOPE], with_swapped(wq[..., NOPE:], 2)], axis=2).astype(bf16)
    wkv_h = w_ukv.reshape(LATENT, N_HEADS, NOPE + V_DIM).transpose(1, 0, 2).astype(bf16)

    def head_gain(g):
        g1, g2 = g[NOPE:NOPE + HALF], g[NOPE + HALF:]
        z = jnp.zeros((LANES - ROPE,), g.dtype)
        return jnp.concatenate([g[:NOPE], g1, g2, z, -g2, g1, z])[None, :]

    return dict(w_lat=w_lat, w_rest=w_rest, w_kpe=w_kpe, wq_h=wq_h, wkv_h=wkv_h,
                gqh=head_gain(g_q_head), gkh=head_gain(g_k_head),
                wa=w_proj_mla.astype(bf16), wb=w_proj_sb.astype(bf16), wo=w_out.astype(bf16),
                w_ffn=w_ffn_in.astype(bf16), wf=w_ffn_out.astype(bf16))


def kernel(x, c, positions, w_ada, b_ada, g_norm1, g_norm2, w_in, g_q_latent, g_kv_latent, w_uq, w_ukv,
           g_q_head, g_k_head, w_proj_mla, w_proj_sb, w_out, w_ffn_in, w_ffn_out):
    batch, seq, d = x.shape
    depth = w_ada.shape[0]
    t = batch * seq
    x2 = x.reshape(t, d)
    pos2 = positions.reshape(t, 1)
    c_pad = jnp.pad(c, ((0, 8 - batch), (0, 0)))
    lane = np.arange(LANES)
    freq = jnp.where(lane < ROPE, ROPE_THETA ** (-jnp.asarray(lane % HALF, f32) / HALF), 0.0)[None, :]

    for l in range(depth):
        p = _prep_layer(w_in[l], w_uq[l], w_ukv[l], g_q_head[l], g_k_head[l], w_proj_mla[l], w_proj_sb[l],
                        w_out[l], w_ffn_in[l], w_ffn_out[l])
        ada = _ada(c_pad, w_ada[l], b_ada[l][None, :])
        ada3 = ada[:batch].reshape(batch, 6, d)
        proj, kpe = _in_proj(x2, ada3, g_norm1[l][None, :], p["w_lat"], p["w_rest"], p["w_kpe"], seq)
        shift = _score_shift(g_q_head[l], g_k_head[l])
        use_shift = shift <= SHIFT_MAX
        pad_lane = (lane == ROPE).astype(f32)[None, :]
        padq = pad_lane * use_shift.astype(f32)
        padk = pad_lane * jnp.where(use_shift, -shift, 0.0)
        q, k, vt = _latent_up(proj, kpe, pos2, freq, g_q_latent[l][None, :], g_kv_latent[l][None, :],
                              p["wq_h"], p["wkv_h"], p["gqh"], p["gkh"], padq, padk, batch, seq)
        ya = lax.cond(use_shift, functools.partial(_mla_attn, shifted=True),
                      functools.partial(_mla_attn, shifted=False), q, k, vt)
        yb = _sb_attn(proj, batch, seq)
        x1 = _merge_out(ya, yb, proj, x2, ada3, p["wa"], p["wb"], p["wo"], seq)
        x2 = _ffn(x1, ada3, g_norm2[l][None, :], p["w_ffn"], p["wf"], seq)
    return x2.reshape(batch, seq, d)
```

```python
import functools

import jax
import jax.numpy as jnp
import numpy as np
from jax import lax
from jax.experimental import pallas as pl
from jax.experimental.pallas import tpu as pltpu

D_MODEL = 2048
N_HEADS = 8
LATENT = 512
NOPE = 128
ROPE = 64
HALF = ROPE // 2
QK_DIM = NOPE + ROPE
QK_PAD = 256
V_DIM = 128
SB_DIM = 128
SUB = 128
MLA_HEADS_PER_STEP = 4
SB_HEADS_PER_STEP = 8
HEADS_W = N_HEADS * 128
ROPE_THETA = 10000.0
EPS = 1e-6
LANES = 128
NEG_BIG = -1e30
LOG2_E = 1.4426950408889634
SB_DEAD_LOG2 = -150.0
SHIFT_MAX = 60.0

COL_CQ, COL_CKV, COL_QSB, COL_KSB, COL_VSB, COL_GA, COL_GB = 0, 512, 1024, 2048, 3072, 4096, 6144
PROJ_W = 8192

VMEM_LIMIT = 56 * 1024 * 1024
BF16_SUBLANES = 16

f32 = jnp.float32
bf16 = jnp.bfloat16


def _cparams(sem):
    return pltpu.CompilerParams(dimension_semantics=sem, vmem_limit_bytes=VMEM_LIMIT)


def _dot(a, b):
    return jnp.dot(a, b, preferred_element_type=f32)


def _dot_nt(a, b):
    return lax.dot_general(a, b, (((1,), (1,)), ((), ())), preferred_element_type=f32)


def _rms_rows(xf, g_row):
    return xf * lax.rsqrt(jnp.mean(xf * xf, axis=-1, keepdims=True) + EPS) * g_row


def _ada_kernel(c_ref, w_ref, b_ref, o_ref):
    c = c_ref[...]
    c_act = (c * (1.0 / (1.0 + jnp.exp(-c)))).astype(bf16)
    o_ref[...] = _dot(c_act, w_ref[...].astype(bf16)) + b_ref[...]


def _ada(c_pad, w_ada, b_ada):
    rows, d = c_pad.shape
    n = w_ada.shape[1]
    tn = 1024
    return pl.pallas_call(
        _ada_kernel,
        out_shape=jax.ShapeDtypeStruct((rows, n), f32),
        grid=(n // tn,),
        in_specs=[pl.BlockSpec((rows, d), lambda j: (0, 0)),
                  pl.BlockSpec((d, tn), lambda j: (0, j)),
                  pl.BlockSpec((1, tn), lambda j: (0, j))],
        out_specs=pl.BlockSpec((rows, tn), lambda j: (0, j)),
        compiler_params=_cparams(("arbitrary",)),
        name="ada",
    )(c_pad, w_ada, b_ada)


def _in_proj_kernel(x_ref, ada_ref, g_ref, wl_ref, w_ref, wk_ref, *rest, n_side):
    side_in = rest[:n_side]
    proj_ref, kpe_ref = rest[n_side], rest[n_side + 1]
    side_out = rest[n_side + 2:2 * n_side + 2]
    h_ref = rest[2 * n_side + 2]
    n = pl.program_id(1)

    def cast_side():
        for src, dst in zip(side_in, side_out):
            dst[...] = src[...].astype(bf16)

    @pl.when(n == 0)
    def _():
        shift = ada_ref[0, 0:1, :]
        scale = ada_ref[0, 1:2, :]
        h = _rms_rows(x_ref[...], g_ref[...]) * (1.0 + scale) + shift
        h_ref[...] = h.astype(bf16)
        kpe_ref[...] = _dot(h_ref[...], wk_ref[...])
        proj_ref[...] = _dot(h_ref[...], wl_ref[...]).astype(bf16)
        cast_side()

    @pl.when(n > 0)
    def _():
        proj_ref[...] = _dot(h_ref[...], w_ref[...]).astype(bf16)
        cast_side()


def _in_proj(x2, ada3, g1, w_lat, w_rest, w_kpe, side, seq):
    t, d = x2.shape
    tm = 1024
    tn = w_lat.shape[1]
    per_b = seq // tm
    n_n = PROJ_W // tn
    n_steps = (t // tm) * n_n

    def side_spec(a):
        rows = a.shape[0]
        hold = next(k for k in (1, 2, 4, 8) if (rows * k) % (n_steps * BF16_SUBLANES) == 0)
        return pl.BlockSpec((rows * hold // n_steps, a.shape[1]), lambda i, n: ((i * n_n + n) // hold, 0))
    outs = pl.pallas_call(
        functools.partial(_in_proj_kernel, n_side=len(side)),
        out_shape=(jax.ShapeDtypeStruct((t, PROJ_W), bf16), jax.ShapeDtypeStruct((t, 2 * LANES), f32),
                   *[jax.ShapeDtypeStruct(a.shape, bf16) for a in side]),
        grid=(t // tm, n_n),
        in_specs=[pl.BlockSpec((tm, d), lambda i, n: (i, 0)),
                  pl.BlockSpec((1, 6, d), lambda i, n: (i // per_b, 0, 0)),
                  pl.BlockSpec((1, d), lambda i, n: (0, 0)),
                  pl.BlockSpec((d, tn), lambda i, n: (0, 0)),
                  pl.BlockSpec((d, tn), lambda i, n: (0, jnp.maximum(n - 1, 0))),
                  pl.BlockSpec((d, 2 * LANES), lambda i, n: (0, 0)),
                  *[side_spec(a) for a in side]],
        out_specs=(pl.BlockSpec((tm, tn), lambda i, n: (i, n)),
                   pl.BlockSpec((tm, 2 * LANES), lambda i, n: (i, 0)),
                   *[side_spec(a) for a in side]),
        scratch_shapes=[pltpu.VMEM((tm, d), bf16)],
        compiler_params=_cparams(("arbitrary", "arbitrary")),
        name="in_proj",
    )(x2, ada3, g1, w_lat, w_rest, w_kpe, *side)
    return outs[0], outs[1], outs[2:]


def _row_sumsq(x):
    return _dot((x * x).astype(bf16), jnp.ones((x.shape[1], LANES), bf16))


def _latent_up_kernel(cq_ref, ckv_ref, kpe_ref, pos_ref, freq_ref, gql_ref, gkvl_ref, wq_ref, wkv_ref,
                      gqh_ref, gkh_ref, padq_ref, padk_ref, q_ref, k_ref, vt_ref,
                      cqn_ref, ckvn_ref, cos_ref, sin_ref, kss_ref):
    cqn_ref[...] = _rms_rows(cq_ref[...].astype(f32), gql_ref[...]).astype(bf16)
    ckvn_ref[...] = _rms_rows(ckv_ref[...].astype(f32), gkvl_ref[...]).astype(bf16)
    ang = pos_ref[...].astype(f32) * freq_ref[...]
    cos_ref[...] = jnp.cos(ang)
    sin_ref[...] = jnp.sin(ang)
    kss_ref[...] = _row_sumsq(kpe_ref[:, 0:LANES])
    gq = gqh_ref[...] * (LOG2_E * QK_DIM ** -0.5)
    gk = gkh_ref[...]

    def head(h, _):
        cos_t, sin_t = cos_ref[...], sin_ref[...]
        q = _dot(cqn_ref[...], wq_ref[h])
        q_r = lax.rsqrt(_row_sumsq(q[:, 0:QK_PAD]) * (1.0 / QK_DIM) + EPS)
        q_ref[0, h, :, 0:NOPE] = (q[:, 0:NOPE] * q_r * gq[:, 0:NOPE]).astype(bf16)
        q_ref[0, h, :, NOPE:QK_PAD] = (q[:, NOPE:QK_PAD] * q_r * gq[:, NOPE:QK_PAD] * cos_t
                                       + q[:, QK_PAD:] * q_r * gq[:, QK_PAD:] * sin_t + padq_ref[...]).astype(bf16)

        kv = _dot(ckvn_ref[...], wkv_ref[h])
        k_nope = kv[:, 0:NOPE]
        k_r = lax.rsqrt((_row_sumsq(k_nope) + kss_ref[...]) * (1.0 / QK_DIM) + EPS)
        k_ref[0, h, :, 0:NOPE] = (k_nope * k_r * gk[:, 0:NOPE]).astype(bf16)
        k_ref[0, h, :, NOPE:QK_PAD] = (kpe_ref[:, 0:LANES] * k_r * gk[:, NOPE:QK_PAD] * cos_t
                                       + kpe_ref[:, LANES:] * k_r * gk[:, QK_PAD:] * sin_t + padk_ref[...]).astype(bf16)
        vt_ref[0, h] = kv[:, NOPE:QK_PAD].T.astype(bf16)
        return 0

    lax.fori_loop(0, N_HEADS, head, 0, unroll=2)


def _latent_up(proj, kpe, pos2, freq, gql, gkvl, wq_h, wkv_h, gqh, gkh, padq, padk, batch, seq):
    t = proj.shape[0]
    tm = 512
    per_b = seq // tm
    q_w = wq_h.shape[2]
    heads_out = lambda w: pl.BlockSpec((1, N_HEADS, tm, w), lambda i: (i // per_b, 0, i % per_b, 0))
    row = lambda w: pl.BlockSpec((1, w), lambda i: (0, 0))
    return pl.pallas_call(
        _latent_up_kernel,
        out_shape=(jax.ShapeDtypeStruct((batch, N_HEADS, seq, QK_PAD), bf16),
                   jax.ShapeDtypeStruct((batch, N_HEADS, seq, QK_PAD), bf16),
                   jax.ShapeDtypeStruct((batch, N_HEADS, V_DIM, seq), bf16)),
        grid=(t // tm,),
        in_specs=[pl.BlockSpec((tm, LATENT), lambda i: (i, COL_CQ // LATENT)),
                  pl.BlockSpec((tm, LATENT), lambda i: (i, COL_CKV // LATENT)),
                  pl.BlockSpec((tm, 2 * LANES), lambda i: (i, 0)),
                  pl.BlockSpec((tm, 1), lambda i: (i, 0)),
                  row(LANES), row(LATENT), row(LATENT),
                  pl.BlockSpec((N_HEADS, LATENT, q_w), lambda i: (0, 0, 0)),
                  pl.BlockSpec((N_HEADS, LATENT, QK_PAD), lambda i: (0, 0, 0)),
                  row(q_w), row(q_w), row(LANES), row(LANES)],
        out_specs=(heads_out(QK_PAD), heads_out(QK_PAD),
                   pl.BlockSpec((1, N_HEADS, V_DIM, tm), lambda i: (i // per_b, 0, 0, i % per_b))),
        scratch_shapes=[pltpu.VMEM((tm, LATENT), bf16), pltpu.VMEM((tm, LATENT), bf16),
                        pltpu.VMEM((tm, LANES), f32), pltpu.VMEM((tm, LANES), f32), pltpu.VMEM((tm, LANES), f32)],
        compiler_params=_cparams(("arbitrary",)),
        name="latent_up",
    )(proj, proj, kpe, pos2, freq, gql, gkvl, wq_h, wkv_h, gqh, gkh, padq, padk)


def _mla_kernel(q_ref, k_ref, vt_ref, o_ref, acc_ref, *, tq, tk, g):
    qi = pl.program_id(2)

    def step(off, carry, masked):
        out = []
        scores = [_dot_nt(k_ref[0, hh, pl.ds(off, tk), :], q_ref[0, hh]) for hh in range(g)]
        for hh in range(g):
            m, l = carry[hh]
            s = scores[hh]
            if masked:
                key = lax.broadcasted_iota(jnp.int32, s.shape, 0)
                qry = lax.broadcasted_iota(jnp.int32, s.shape, 1)
                s = jnp.where(key <= qry, s, NEG_BIG)
            m_new = jnp.maximum(m, jnp.max(s, axis=0, keepdims=True))
            p = jnp.exp2(s - m_new)
            alpha = jnp.exp2(m - m_new)
            l_new = alpha * l + jnp.sum(p, axis=0, keepdims=True)
            acc_ref[hh] = alpha * acc_ref[hh] + _dot(vt_ref[0, hh, :, pl.ds(off, tk)], p.astype(bf16))
            out.append((m_new, l_new))
        return tuple(out)

    acc_ref[...] = jnp.zeros_like(acc_ref)
    init = tuple((jnp.full((1, tq), NEG_BIG, f32), jnp.zeros((1, tq), f32)) for _ in range(g))
    carry = lax.fori_loop(0, qi, lambda j, c: step(pl.multiple_of(j * tk, tk), c, False), init)
    carry = step(pl.multiple_of(qi * tk, tk), carry, True)
    for hh in range(g):
        o_ref[:, hh * V_DIM:(hh + 1) * V_DIM] = (acc_ref[hh] / carry[hh][1]).T.astype(bf16)


def _mla_shifted_kernel(q_ref, k_ref, vt_ref, o_ref, acc_ref, *, tq, tk, g):
    qi = pl.program_id(2)

    def step(off, ls, masked):
        out = []
        scores = [_dot_nt(k_ref[0, hh, pl.ds(off, tk), :], q_ref[0, hh]) for hh in range(g)]
        for hh in range(g):
            s = scores[hh]
            if masked:
                key = lax.broadcasted_iota(jnp.int32, s.shape, 0)
                qry = lax.broadcasted_iota(jnp.int32, s.shape, 1)
                s = jnp.where(key <= qry, s, NEG_BIG)
            p = jnp.exp2(s)
            acc_ref[hh] += _dot(vt_ref[0, hh, :, pl.ds(off, tk)], p.astype(bf16))
            out.append(ls[hh] + jnp.sum(p, axis=0, keepdims=True))
        return tuple(out)

    acc_ref[...] = jnp.zeros_like(acc_ref)
    init = tuple(jnp.zeros((1, tq), f32) for _ in range(g))
    ls = lax.fori_loop(0, qi, lambda j, c: step(pl.multiple_of(j * tk, tk), c, False), init)
    ls = step(pl.multiple_of(qi * tk, tk), ls, True)
    for hh in range(g):
        o_ref[:, hh * V_DIM:(hh + 1) * V_DIM] = (acc_ref[hh] / ls[hh]).T.astype(bf16)


def _mla_attn(q, k, vt, shifted):
    batch, heads, seq, _ = q.shape
    tq = tk = 512
    g = MLA_HEADS_PER_STEP
    nq = seq // tq
    body = _mla_shifted_kernel if shifted else _mla_kernel
    return pl.pallas_call(
        functools.partial(body, tq=tq, tk=tk, g=g),
        out_shape=jax.ShapeDtypeStruct((batch * seq, heads * V_DIM), bf16),
        grid=(batch, heads // g, nq),
        in_specs=[pl.BlockSpec((1, g, tq, QK_PAD), lambda b, h, i: (b, h, i, 0)),
                  pl.BlockSpec((1, g, seq, QK_PAD), lambda b, h, i: (b, h, 0, 0)),
                  pl.BlockSpec((1, g, V_DIM, seq), lambda b, h, i: (b, h, 0, 0))],
        out_specs=pl.BlockSpec((tq, g * V_DIM), lambda b, h, i: (b * nq + i, h)),
        scratch_shapes=[pltpu.VMEM((g, V_DIM, tq), f32)],
        compiler_params=_cparams(("arbitrary", "arbitrary", "arbitrary")),
        name="mla_attn_shifted" if shifted else "mla_attn",
    )(q, k, vt)


def _sb_kernel(q_ref, k_ref, v_ref, o_ref, acc_ref, *, tq, tk, g):
    qi = pl.program_id(2)
    n_sub = tk // SUB
    ss = lax.broadcasted_iota(jnp.int32, (SUB, SUB), 0)
    jj = lax.broadcasted_iota(jnp.int32, (SUB, SUB), 1)
    later = jnp.where(jj > ss, -1.0, 0.0).astype(bf16)
    later2 = jnp.concatenate([later, later], axis=1)
    head_cols = [slice(hh * SB_DIM, (hh + 1) * SB_DIM) for hh in range(g)]
    qs = [(q_ref[:, cols].astype(f32) * (LOG2_E * SB_DIM ** -0.5)).astype(bf16) for cols in head_cols]

    sub_rows = [slice(sb * SUB, (sb + 1) * SUB) for sb in range(n_sub)]

    def step(off, carry, masked):
        zs = [_dot_nt(k_ref[pl.ds(off, tk), head_cols[hh]], qs[hh]) for hh in range(g)]
        if masked:
            key = lax.broadcasted_iota(jnp.int32, (tk, tq), 0)
            qry = lax.broadcasted_iota(jnp.int32, (tk, tq), 1)
            keep = key < qry
        nloms, log_betas, tails = [], [], []
        for hh in range(g):
            z = zs[hh]
            nlom = jnp.maximum(z, 0.0) + jnp.log(1.0 + jnp.exp2(-jnp.abs(z))) * LOG2_E
            log_betas.append(z - nlom)
            if masked:
                nlom = jnp.where(keep, nlom, 0.0)
            nloms.append(nlom)
            head_tails = []
            for rows in sub_rows:
                nlom_sb = nlom[rows, :]
                hi = nlom_sb.astype(bf16)
                lo = (nlom_sb - hi.astype(f32)).astype(bf16)
                head_tails.append(_dot(later2, jnp.concatenate([hi, lo], axis=0)))
            tails.append(head_tails)
        out = []
        for hh in range(g):
            c = carry[hh]
            a_parts = [None] * n_sub
            for sb in reversed(range(n_sub)):
                rows = sub_rows[sb]
                tail = tails[hh][sb]
                a_sb = jnp.exp2(log_betas[hh][rows, :] + tail + c)
                if masked:
                    a_sb = jnp.where(keep[rows, :], a_sb, 0.0)
                a_parts[sb] = a_sb.astype(bf16)
                c = c + (tail[0:1, :] - nloms[hh][rows, :][0:1, :])
            a = jnp.concatenate(a_parts, axis=0)
            acc_ref[hh] += lax.dot_general(v_ref[pl.ds(off, tk), head_cols[hh]], a, (((0,), (0,)), ((), ())),
                                           preferred_element_type=f32)
            out.append(c)
        return tuple(out)

    def any_live(carry):
        c_max = carry[0]
        for c in carry[1:]:
            c_max = jnp.maximum(c_max, c)
        return jnp.max(c_max) > SB_DEAD_LOG2

    acc_ref[...] = jnp.zeros_like(acc_ref)
    carry = step(pl.multiple_of(qi * tk, tk), tuple(jnp.zeros((1, tq), f32) for _ in range(g)), True)

    def body(state):
        t, carry, _ = state
        carry = step(pl.multiple_of((qi - 1 - t) * tk, tk), carry, False)
        return t + 1, carry, any_live(carry)

    lax.while_loop(lambda s: (s[0] < qi) & s[2], body, (jnp.int32(0), carry, any_live(carry)))
    for hh in range(g):
        o_ref[:, hh * SB_DIM:(hh + 1) * SB_DIM] = acc_ref[hh].T.astype(bf16)


def _sb_attn(proj, batch, seq):
    tq = tk = 256
    g = SB_HEADS_PER_STEP
    w = g * SB_DIM
    nq = seq // tq
    q0, k0, v0 = COL_QSB // w, COL_KSB // w, COL_VSB // w
    return pl.pallas_call(
        functools.partial(_sb_kernel, tq=tq, tk=tk, g=g),
        out_shape=jax.ShapeDtypeStruct((batch * seq, HEADS_W), bf16),
        grid=(batch, N_HEADS // g, nq),
        in_specs=[pl.BlockSpec((tq, w), lambda b, h, i: (b * nq + i, q0 + h)),
                  pl.BlockSpec((seq, w), lambda b, h, i: (b, k0 + h)),
                  pl.BlockSpec((seq, w), lambda b, h, i: (b, v0 + h))],
        out_specs=pl.BlockSpec((tq, w), lambda b, h, i: (b * nq + i, h)),
        scratch_shapes=[pltpu.VMEM((g, SB_DIM, tq), f32)],
        compiler_params=_cparams(("arbitrary", "arbitrary", "arbitrary")),
        name="sb_attn",
    )(proj, proj, proj)


def _merge_out_kernel(ya_ref, yb_ref, ga_ref, gb_ref, x_ref, ada_ref, wa_ref, wb_ref, wo_ref, x1_ref):
    sig = lambda ref: 1.0 / (1.0 + jnp.exp(-ref[...].astype(f32)))
    merged = sig(ga_ref) * _dot(ya_ref[...], wa_ref[...]) + sig(gb_ref) * _dot(yb_ref[...], wb_ref[...])
    x1_ref[...] = x_ref[...] + ada_ref[0, 2:3, :] * _dot(merged.astype(bf16), wo_ref[...])


def _merge_out(ya, yb, proj, x2, ada3, wa, wb, wo, seq):
    t, d = x2.shape
    tm = 256
    per_b = seq // tm
    const = lambda shape: pl.BlockSpec(shape, lambda i: (0, 0), pipeline_mode=pl.Buffered(1))
    return pl.pallas_call(
        _merge_out_kernel,
        out_shape=jax.ShapeDtypeStruct((t, d), f32),
        grid=(t // tm,),
        in_specs=[pl.BlockSpec((tm, HEADS_W), lambda i: (i, 0)),
                  pl.BlockSpec((tm, HEADS_W), lambda i: (i, 0)),
                  pl.BlockSpec((tm, d), lambda i: (i, COL_GA // d)),
                  pl.BlockSpec((tm, d), lambda i: (i, COL_GB // d)),
                  pl.BlockSpec((tm, d), lambda i: (i, 0)),
                  pl.BlockSpec((1, 6, d), lambda i: (i // per_b, 0, 0)),
                  const((HEADS_W, d)), const((HEADS_W, d)), const((d, d))],
        out_specs=pl.BlockSpec((tm, d), lambda i: (i, 0)),
        compiler_params=_cparams(("arbitrary",)),
        name="merge_out",
    )(ya, yb, proj, proj, x2, ada3, wa, wb, wo)


def _ffn_kernel(x1_ref, ada_ref, g2_ref, wg_ref, wu_ref, wo_ref, o_ref, h_ref):
    f = pl.program_id(1)

    @pl.when(f == 0)
    def _():
        x1 = x1_ref[...]
        d = x1.shape[1]
        inv = lax.rsqrt(_row_sumsq(x1) * (1.0 / d) + EPS)
        inv = jnp.concatenate([inv] * (d // LANES), axis=1)
        h = x1 * inv * (g2_ref[...] * (1.0 + ada_ref[0, 4:5, :])) + ada_ref[0, 3:4, :]
        h_ref[...] = h.astype(bf16)
        o_ref[...] = jnp.zeros_like(o_ref)

    h = h_ref[...]
    gate = _dot(h, wg_ref[...])
    up = _dot(h, wu_ref[...])
    act = gate * (1.0 / (1.0 + jnp.exp(-gate))) * up
    o_ref[...] += _dot(act.astype(bf16), wo_ref[...])

    @pl.when(f == pl.num_programs(1) - 1)
    def _():
        o_ref[...] = x1_ref[...] + ada_ref[0, 5:6, :] * o_ref[...]


def _ffn(x1, ada3, g2, w_in, wo, seq):
    t, d = x1.shape
    d_ff = wo.shape[0]
    tm, tf = 1024, 512
    per_b = seq // tm
    n_f = d_ff // tf
    return pl.pallas_call(
        _ffn_kernel,
        out_shape=jax.ShapeDtypeStruct((t, d), f32),
        grid=(t // tm, n_f),
        in_specs=[pl.BlockSpec((tm, d), lambda i, f: (i, 0)),
                  pl.BlockSpec((1, 6, d), lambda i, f: (i // per_b, 0, 0)),
                  pl.BlockSpec((1, d), lambda i, f: (0, 0)),
                  pl.BlockSpec((d, tf), lambda i, f: (0, f)),
                  pl.BlockSpec((d, tf), lambda i, f: (0, n_f + f)),
                  pl.BlockSpec((tf, d), lambda i, f: (f, 0))],
        out_specs=pl.BlockSpec((tm, d), lambda i, f: (i, 0)),
        scratch_shapes=[pltpu.VMEM((tm, d), bf16)],
        compiler_params=_cparams(("arbitrary", "arbitrary")),
        name="ffn",
    )(x1, ada3, g2, w_in, w_in, wo)


def _score_shift(g_q_head, g_k_head):
    c = LOG2_E * QK_DIM ** -0.5
    return 1.02 * QK_DIM * c * jnp.max(jnp.abs(g_q_head)) * jnp.max(jnp.abs(g_k_head))


def _prep_layer(w_in, w_uq, w_ukv, g_q_head, g_k_head, w_proj_mla, w_proj_sb, w_out, w_ffn_in, w_ffn_out):
    o_kpe, o_qsb = 2 * LATENT, 2 * LATENT + ROPE

    def with_swapped(w, axis):
        x1, x2 = jnp.split(w, 2, axis=axis)
        shape = list(w.shape)
        shape[axis] = LANES - ROPE
        z = jnp.zeros(shape, w.dtype)
        return jnp.concatenate([x1, x2, z, x2, x1, z], axis=axis)

    w_lat = w_in[:, :o_kpe].astype(bf16)
    w_rest = w_in[:, o_qsb:].astype(bf16)
    w_kpe = with_swapped(w_in[:, o_kpe:o_qsb], 1).astype(bf16)
    wq = w_uq.reshape(LATENT, N_HEADS, QK_DIM).transpose(1, 0, 2)
    wq_h = jnp.concatenate([wq[..., :NOPE], with_swapped(wq[..., NOPE:], 2)], axis=2).astype(bf16)
    wkv_h = w_ukv.reshape(LATENT, N_HEADS, NOPE + V_DIM).transpose(1, 0, 2).astype(bf16)

    def head_gain(g):
        g1, g2 = g[NOPE:NOPE + HALF], g[NOPE + HALF:]
        z = jnp.zeros((LANES - ROPE,), g.dtype)
        return jnp.concatenate([g[:NOPE], g1, g2, z, -g2, g1, z])[None, :]

    return dict(w_lat=w_lat, w_rest=w_rest, w_kpe=w_kpe, wq_h=wq_h, wkv_h=wkv_h,
                gqh=head_gain(g_q_head), gkh=head_gain(g_k_head),
                side=(w_proj_mla, w_proj_sb, w_out, w_ffn_in, w_ffn_out))


def kernel(x, c, positions, w_ada, b_ada, g_norm1, g_norm2, w_in, g_q_latent, g_kv_latent, w_uq, w_ukv,
           g_q_head, g_k_head, w_proj_mla, w_proj_sb, w_out, w_ffn_in, w_ffn_out):
    batch, seq, d = x.shape
    depth = w_ada.shape[0]
    t = batch * seq
    x2 = x.reshape(t, d)
    pos2 = positions.reshape(t, 1)
    c_pad = jnp.pad(c, ((0, 8 - batch), (0, 0)))
    lane = np.arange(LANES)
    freq = jnp.where(lane < ROPE, ROPE_THETA ** (-jnp.asarray(lane % HALF, f32) / HALF), 0.0)[None, :]

    for l in range(depth):
        p = _prep_layer(w_in[l], w_uq[l], w_ukv[l], g_q_head[l], g_k_head[l], w_proj_mla[l], w_proj_sb[l],
                        w_out[l], w_ffn_in[l], w_ffn_out[l])
        ada = _ada(c_pad, w_ada[l], b_ada[l][None, :])
        ada3 = ada[:batch].reshape(batch, 6, d)
        proj, kpe, (wa, wb, wo, w_ffn, wf) = _in_proj(x2, ada3, g_norm1[l][None, :], p["w_lat"], p["w_rest"],
                                                      p["w_kpe"], p["side"], seq)
        shift = _score_shift(g_q_head[l], g_k_head[l])
        use_shift = shift <= SHIFT_MAX
        pad_lane = (lane == ROPE).astype(f32)[None, :]
        padq = pad_lane * use_shift.astype(f32)
        padk = pad_lane * jnp.where(use_shift, -shift, 0.0)
        q, k, vt = _latent_up(proj, kpe, pos2, freq, g_q_latent[l][None, :], g_kv_latent[l][None, :],
                              p["wq_h"], p["wkv_h"], p["gqh"], p["gkh"], padq, padk, batch, seq)
        ya = lax.cond(use_shift, functools.partial(_mla_attn, shifted=True),
                      functools.partial(_mla_attn, shifted=False), q, k, vt)
        yb = _sb_attn(proj, batch, seq)
        x1 = _merge_out(ya, yb, proj, x2, ada3, wa, wb, wo, seq)
        x2 = _ffn(x1, ada3, g_norm2[l][None, :], w_ffn, wf, seq)
    return x2.reshape(batch, seq, d)
```

```python
import functools

import jax
import jax.numpy as jnp
import numpy as np
from jax import lax
from jax.experimental import pallas as pl
from jax.experimental.pallas import tpu as pltpu

D_MODEL = 2048
N_HEADS = 8
LATENT = 512
NOPE = 128
ROPE = 64
HALF = ROPE // 2
QK_DIM = NOPE + ROPE
QK_PAD = 256
V_DIM = 128
SB_DIM = 128
SUB = 128
MLA_HEADS_PER_STEP = 4
SB_HEADS_PER_STEP = 8
HEADS_W = N_HEADS * 128
ROPE_THETA = 10000.0
EPS = 1e-6
LANES = 128
NEG_BIG = -1e30
LOG2_E = 1.4426950408889634
SB_DEAD_LOG2 = -150.0
SHIFT_MAX = 60.0

COL_CQ, COL_CKV, COL_QSB, COL_KSB, COL_VSB, COL_GA, COL_GB = 0, 512, 1024, 2048, 3072, 4096, 6144
PROJ_W = 8192

VMEM_LIMIT = 56 * 1024 * 1024
BF16_SUBLANES = 16
F32_SUBLANES = 8

f32 = jnp.float32
bf16 = jnp.bfloat16


def _cparams(sem):
    return pltpu.CompilerParams(dimension_semantics=sem, vmem_limit_bytes=VMEM_LIMIT)


def _dot(a, b):
    return jnp.dot(a, b, preferred_element_type=f32)


def _dot_nt(a, b):
    return lax.dot_general(a, b, (((1,), (1,)), ((), ())), preferred_element_type=f32)


def _rms_rows(xf, g_row):
    return xf * lax.rsqrt(jnp.mean(xf * xf, axis=-1, keepdims=True) + EPS) * g_row


def _ada_kernel(c_ref, w_ref, b_ref, wint_ref, o_ref, wmain_ref, *, n_w):
    c = c_ref[...]
    c_act = (c * (1.0 / (1.0 + jnp.exp(-c)))).astype(bf16)
    o_ref[...] = _dot(c_act, w_ref[...].astype(bf16)) + b_ref[...]

    @pl.when(pl.program_id(0) < n_w)
    def _():
        wmain_ref[...] = wint_ref[...].T.astype(bf16)


def _ada(c_pad, w_ada, b_ada, w_in_t):
    rows, d = c_pad.shape
    n = w_ada.shape[1]
    tn = 768
    tw = 2 * LATENT
    o_rest = 2 * LATENT + ROPE
    n_rest = (w_in_t.shape[0] - o_rest) // tw
    n_w = n_rest + 1
    assert n // tn >= n_w and n_w * tw == PROJ_W
    return pl.pallas_call(
        functools.partial(_ada_kernel, n_w=n_w),
        out_shape=(jax.ShapeDtypeStruct((rows, n), f32), jax.ShapeDtypeStruct((d, PROJ_W), bf16)),
        grid=(n // tn,),
        in_specs=[pl.BlockSpec((rows, d), lambda j: (0, 0)),
                  pl.BlockSpec((d, tn), lambda j: (0, j)),
                  pl.BlockSpec((1, tn), lambda j: (0, j)),
                  pl.BlockSpec((pl.Element(tw), pl.Element(d)),
                               lambda j: (F32_SUBLANES * jnp.where(j < n_rest, (o_rest + tw * j) // F32_SUBLANES, 0), 0))],
        out_specs=(pl.BlockSpec((rows, tn), lambda j: (0, j)),
                   pl.BlockSpec((d, tw), lambda j: (0, jnp.where(j < n_rest, j + 1, 0)))),
        compiler_params=_cparams(("arbitrary",)),
        name="ada",
    )(c_pad, w_ada, b_ada, w_in_t)


def _in_proj_kernel(x_ref, ada_ref, g_ref, w_ref, wk_ref, *rest, n_side):
    side_in = rest[:n_side]
    proj_ref, kpe_ref = rest[n_side], rest[n_side + 1]
    side_out = rest[n_side + 2:2 * n_side + 2]
    h_ref = rest[2 * n_side + 2]
    n = pl.program_id(1)

    def cast_side():
        for src, dst in zip(side_in, side_out):
            dst[...] = src[...].astype(bf16)

    @pl.when(n == 0)
    def _():
        shift = ada_ref[0, 0:1, :]
        scale = ada_ref[0, 1:2, :]
        h = _rms_rows(x_ref[...], g_ref[...]) * (1.0 + scale) + shift
        h_ref[...] = h.astype(bf16)
        kpe_ref[...] = _dot_nt(h_ref[...], wk_ref[...])

    proj_ref[...] = _dot(h_ref[...], w_ref[...]).astype(bf16)
    cast_side()


def _in_proj(x2, ada3, g1, w_main, w_kpe, side, seq):
    t, d = x2.shape
    tm = 1024
    tn = 2 * LATENT
    per_b = seq // tm
    n_n = PROJ_W // tn
    n_steps = (t // tm) * n_n

    def side_spec(a):
        rows = a.shape[0]
        hold = next(k for k in (1, 2, 4, 8) if (rows * k) % (n_steps * BF16_SUBLANES) == 0)
        return pl.BlockSpec((rows * hold // n_steps, a.shape[1]), lambda i, n: ((i * n_n + n) // hold, 0))
    outs = pl.pallas_call(
        functools.partial(_in_proj_kernel, n_side=len(side)),
        out_shape=(jax.ShapeDtypeStruct((t, PROJ_W), bf16), jax.ShapeDtypeStruct((t, 2 * LANES), f32),
                   *[jax.ShapeDtypeStruct(a.shape, bf16) for a in side]),
        grid=(t // tm, n_n),
        in_specs=[pl.BlockSpec((tm, d), lambda i, n: (i, 0)),
                  pl.BlockSpec((1, 6, d), lambda i, n: (i // per_b, 0, 0)),
                  pl.BlockSpec((1, d), lambda i, n: (0, 0)),
                  pl.BlockSpec((d, tn), lambda i, n: (0, n)),
                  pl.BlockSpec((2 * LANES, d), lambda i, n: (0, 0)),
                  *[side_spec(a) for a in side]],
        out_specs=(pl.BlockSpec((tm, tn), lambda i, n: (i, n)),
                   pl.BlockSpec((tm, 2 * LANES), lambda i, n: (i, 0)),
                   *[side_spec(a) for a in side]),
        scratch_shapes=[pltpu.VMEM((tm, d), bf16)],
        compiler_params=_cparams(("arbitrary", "arbitrary")),
        name="in_proj",
    )(x2, ada3, g1, w_main, w_kpe, *side)
    return outs[0], outs[1], outs[2:]


def _row_sumsq(x):
    return _dot((x * x).astype(bf16), jnp.ones((x.shape[1], LANES), bf16))


def _latent_up_kernel(cq_ref, ckv_ref, kpe_ref, pos_ref, freq_ref, gql_ref, gkvl_ref, wq_ref, wkv_ref,
                      gqh_ref, gkh_ref, padq_ref, padk_ref, q_ref, k_ref, vt_ref,
                      cqn_ref, ckvn_ref, cos_ref, sin_ref, kss_ref):
    cqn_ref[...] = _rms_rows(cq_ref[...].astype(f32), gql_ref[...]).astype(bf16)
    ckvn_ref[...] = _rms_rows(ckv_ref[...].astype(f32), gkvl_ref[...]).astype(bf16)
    ang = pos_ref[...].astype(f32) * freq_ref[...]
    cos_ref[...] = jnp.cos(ang)
    sin_ref[...] = jnp.sin(ang)
    kss_ref[...] = _row_sumsq(kpe_ref[:, 0:LANES])
    gq = gqh_ref[...] * (LOG2_E * QK_DIM ** -0.5)
    gk = gkh_ref[...]

    def head(h, _):
        cos_t, sin_t = cos_ref[...], sin_ref[...]
        q = _dot(cqn_ref[...], wq_ref[h])
        q_r = lax.rsqrt(_row_sumsq(q[:, 0:QK_PAD]) * (1.0 / QK_DIM) + EPS)
        q_ref[0, h, :, 0:NOPE] = (q[:, 0:NOPE] * q_r * gq[:, 0:NOPE]).astype(bf16)
        q_ref[0, h, :, NOPE:QK_PAD] = (q[:, NOPE:QK_PAD] * q_r * gq[:, NOPE:QK_PAD] * cos_t
                                       + q[:, QK_PAD:] * q_r * gq[:, QK_PAD:] * sin_t + padq_ref[...]).astype(bf16)

        kv = _dot(ckvn_ref[...], wkv_ref[h])
        k_nope = kv[:, 0:NOPE]
        k_r = lax.rsqrt((_row_sumsq(k_nope) + kss_ref[...]) * (1.0 / QK_DIM) + EPS)
        k_ref[0, h, :, 0:NOPE] = (k_nope * k_r * gk[:, 0:NOPE]).astype(bf16)
        k_ref[0, h, :, NOPE:QK_PAD] = (kpe_ref[:, 0:LANES] * k_r * gk[:, NOPE:QK_PAD] * cos_t
                                       + kpe_ref[:, LANES:] * k_r * gk[:, QK_PAD:] * sin_t + padk_ref[...]).astype(bf16)
        vt_ref[0, h] = kv[:, NOPE:QK_PAD].T.astype(bf16)
        return 0

    lax.fori_loop(0, N_HEADS, head, 0, unroll=2)


def _latent_up(proj, kpe, pos2, freq, gql, gkvl, wq_h, wkv_h, gqh, gkh, padq, padk, batch, seq):
    t = proj.shape[0]
    tm = 512
    per_b = seq // tm
    q_w = wq_h.shape[2]
    heads_out = lambda w: pl.BlockSpec((1, N_HEADS, tm, w), lambda i: (i // per_b, 0, i % per_b, 0))
    row = lambda w: pl.BlockSpec((1, w), lambda i: (0, 0))
    return pl.pallas_call(
        _latent_up_kernel,
        out_shape=(jax.ShapeDtypeStruct((batch, N_HEADS, seq, QK_PAD), bf16),
                   jax.ShapeDtypeStruct((batch, N_HEADS, seq, QK_PAD), bf16),
                   jax.ShapeDtypeStruct((batch, N_HEADS, V_DIM, seq), bf16)),
        grid=(t // tm,),
        in_specs=[pl.BlockSpec((tm, LATENT), lambda i: (i, COL_CQ // LATENT)),
                  pl.BlockSpec((tm, LATENT), lambda i: (i, COL_CKV // LATENT)),
                  pl.BlockSpec((tm, 2 * LANES), lambda i: (i, 0)),
                  pl.BlockSpec((tm, 1), lambda i: (i, 0)),
                  row(LANES), row(LATENT), row(LATENT),
                  pl.BlockSpec((N_HEADS, LATENT, q_w), lambda i: (0, 0, 0)),
                  pl.BlockSpec((N_HEADS, LATENT, QK_PAD), lambda i: (0, 0, 0)),
                  row(q_w), row(q_w), row(LANES), row(LANES)],
        out_specs=(heads_out(QK_PAD), heads_out(QK_PAD),
                   pl.BlockSpec((1, N_HEADS, V_DIM, tm), lambda i: (i // per_b, 0, 0, i % per_b))),
        scratch_shapes=[pltpu.VMEM((tm, LATENT), bf16), pltpu.VMEM((tm, LATENT), bf16),
                        pltpu.VMEM((tm, LANES), f32), pltpu.VMEM((tm, LANES), f32), pltpu.VMEM((tm, LANES), f32)],
        compiler_params=_cparams(("arbitrary",)),
        name="latent_up",
    )(proj, proj, kpe, pos2, freq, gql, gkvl, wq_h, wkv_h, gqh, gkh, padq, padk)


def _mla_kernel(q_ref, k_ref, vt_ref, o_ref, acc_ref, *, tq, tk, g):
    qi = pl.program_id(2)

    def step(off, carry, masked):
        out = []
        scores = [_dot_nt(k_ref[0, hh, pl.ds(off, tk), :], q_ref[0, hh]) for hh in range(g)]
        for hh in range(g):
            m, l = carry[hh]
            s = scores[hh]
            if masked:
                key = lax.broadcasted_iota(jnp.int32, s.shape, 0)
                qry = lax.broadcasted_iota(jnp.int32, s.shape, 1)
                s = jnp.where(key <= qry, s, NEG_BIG)
            m_new = jnp.maximum(m, jnp.max(s, axis=0, keepdims=True))
            p = jnp.exp2(s - m_new)
            alpha = jnp.exp2(m - m_new)
            l_new = alpha * l + jnp.sum(p, axis=0, keepdims=True)
            acc_ref[hh] = alpha * acc_ref[hh] + _dot(vt_ref[0, hh, :, pl.ds(off, tk)], p.astype(bf16))
            out.append((m_new, l_new))
        return tuple(out)

    acc_ref[...] = jnp.zeros_like(acc_ref)
    init = tuple((jnp.full((1, tq), NEG_BIG, f32), jnp.zeros((1, tq), f32)) for _ in range(g))
    carry = lax.fori_loop(0, qi, lambda j, c: step(pl.multiple_of(j * tk, tk), c, False), init)
    carry = step(pl.multiple_of(qi * tk, tk), carry, True)
    for hh in range(g):
        o_ref[:, hh * V_DIM:(hh + 1) * V_DIM] = (acc_ref[hh] / carry[hh][1]).T.astype(bf16)


def _mla_shifted_kernel(q_ref, k_ref, vt_ref, o_ref, acc_ref, *, tq, tk, g):
    qi = pl.program_id(2)

    def step(off, ls, masked):
        out = []
        scores = [_dot_nt(k_ref[0, hh, pl.ds(off, tk), :], q_ref[0, hh]) for hh in range(g)]
        for hh in range(g):
            s = scores[hh]
            if masked:
                key = lax.broadcasted_iota(jnp.int32, s.shape, 0)
                qry = lax.broadcasted_iota(jnp.int32, s.shape, 1)
                s = jnp.where(key <= qry, s, NEG_BIG)
            p = jnp.exp2(s)
            acc_ref[hh] += _dot(vt_ref[0, hh, :, pl.ds(off, tk)], p.astype(bf16))
            out.append(ls[hh] + jnp.sum(p, axis=0, keepdims=True))
        return tuple(out)

    acc_ref[...] = jnp.zeros_like(acc_ref)
    init = tuple(jnp.zeros((1, tq), f32) for _ in range(g))
    ls = lax.fori_loop(0, qi, lambda j, c: step(pl.multiple_of(j * tk, tk), c, False), init)
    ls = step(pl.multiple_of(qi * tk, tk), ls, True)
    for hh in range(g):
        o_ref[:, hh * V_DIM:(hh + 1) * V_DIM] = (acc_ref[hh] / ls[hh]).T.astype(bf16)


def _mla_attn(q, k, vt, shifted):
    batch, heads, seq, _ = q.shape
    tq = tk = 512
    g = MLA_HEADS_PER_STEP
    nq = seq // tq
    body = _mla_shifted_kernel if shifted else _mla_kernel
    return pl.pallas_call(
        functools.partial(body, tq=tq, tk=tk, g=g),
        out_shape=jax.ShapeDtypeStruct((batch * seq, heads * V_DIM), bf16),
        grid=(batch, heads // g, nq),
        in_specs=[pl.BlockSpec((1, g, tq, QK_PAD), lambda b, h, i: (b, h, i, 0)),
                  pl.BlockSpec((1, g, seq, QK_PAD), lambda b, h, i: (b, h, 0, 0)),
                  pl.BlockSpec((1, g, V_DIM, seq), lambda b, h, i: (b, h, 0, 0))],
        out_specs=pl.BlockSpec((tq, g * V_DIM), lambda b, h, i: (b * nq + i, h)),
        scratch_shapes=[pltpu.VMEM((g, V_DIM, tq), f32)],
        compiler_params=_cparams(("arbitrary", "arbitrary", "arbitrary")),
        name="mla_attn_shifted" if shifted else "mla_attn",
    )(q, k, vt)


def _sb_kernel(q_ref, k_ref, v_ref, o_ref, acc_ref, *, tq, tk, g):
    qi = pl.program_id(2)
    n_sub = tk // SUB
    ss = lax.broadcasted_iota(jnp.int32, (SUB, SUB), 0)
    jj = lax.broadcasted_iota(jnp.int32, (SUB, SUB), 1)
    later = jnp.where(jj > ss, -1.0, 0.0).astype(bf16)
    later2 = jnp.concatenate([later, later], axis=1)
    head_cols = [slice(hh * SB_DIM, (hh + 1) * SB_DIM) for hh in range(g)]
    qs = [(q_ref[:, cols].astype(f32) * (LOG2_E * SB_DIM ** -0.5)).astype(bf16) for cols in head_cols]

    sub_rows = [slice(sb * SUB, (sb + 1) * SUB) for sb in range(n_sub)]

    def step(off, carry, masked):
        zs = [_dot_nt(k_ref[pl.ds(off, tk), head_cols[hh]], qs[hh]) for hh in range(g)]
        if masked:
            key = lax.broadcasted_iota(jnp.int32, (tk, tq), 0)
            qry = lax.broadcasted_iota(jnp.int32, (tk, tq), 1)
            keep = key < qry
        nloms, log_betas, tails = [], [], []
        for hh in range(g):
            z = zs[hh]
            nlom = jnp.maximum(z, 0.0) + jnp.log(1.0 + jnp.exp2(-jnp.abs(z))) * LOG2_E
            log_betas.append(z - nlom)
            if masked:
                nlom = jnp.where(keep, nlom, 0.0)
            nloms.append(nlom)
            head_tails = []
            for rows in sub_rows:
                nlom_sb = nlom[rows, :]
                hi = nlom_sb.astype(bf16)
                lo = (nlom_sb - hi.astype(f32)).astype(bf16)
                head_tails.append(_dot(later2, jnp.concatenate([hi, lo], axis=0)))
            tails.append(head_tails)
        out = []
        for hh in range(g):
            c = carry[hh]
            a_parts = [None] * n_sub
            for sb in reversed(range(n_sub)):
                rows = sub_rows[sb]
                tail = tails[hh][sb]
                a_sb = jnp.exp2(log_betas[hh][rows, :] + tail + c)
                if masked:
                    a_sb = jnp.where(keep[rows, :], a_sb, 0.0)
                a_parts[sb] = a_sb.astype(bf16)
                c = c + (tail[0:1, :] - nloms[hh][rows, :][0:1, :])
            a = jnp.concatenate(a_parts, axis=0)
            acc_ref[hh] += lax.dot_general(v_ref[pl.ds(off, tk), head_cols[hh]], a, (((0,), (0,)), ((), ())),
                                           preferred_element_type=f32)
            out.append(c)
        return tuple(out)

    def any_live(carry):
        c_max = carry[0]
        for c in carry[1:]:
            c_max = jnp.maximum(c_max, c)
        return jnp.max(c_max) > SB_DEAD_LOG2

    acc_ref[...] = jnp.zeros_like(acc_ref)
    carry = step(pl.multiple_of(qi * tk, tk), tuple(jnp.zeros((1, tq), f32) for _ in range(g)), True)

    def body(state):
        t, carry, _ = state
        carry = step(pl.multiple_of((qi - 1 - t) * tk, tk), carry, False)
        return t + 1, carry, any_live(carry)

    lax.while_loop(lambda s: (s[0] < qi) & s[2], body, (jnp.int32(0), carry, any_live(carry)))
    for hh in range(g):
        o_ref[:, hh * SB_DIM:(hh + 1) * SB_DIM] = acc_ref[hh].T.astype(bf16)


def _sb_attn(proj, batch, seq):
    tq = tk = 256
    g = SB_HEADS_PER_STEP
    w = g * SB_DIM
    nq = seq // tq
    q0, k0, v0 = COL_QSB // w, COL_KSB // w, COL_VSB // w
    return pl.pallas_call(
        functools.partial(_sb_kernel, tq=tq, tk=tk, g=g),
        out_shape=jax.ShapeDtypeStruct((batch * seq, HEADS_W), bf16),
        grid=(batch, N_HEADS // g, nq),
        in_specs=[pl.BlockSpec((tq, w), lambda b, h, i: (b * nq + i, q0 + h)),
                  pl.BlockSpec((seq, w), lambda b, h, i: (b, k0 + h)),
                  pl.BlockSpec((seq, w), lambda b, h, i: (b, v0 + h))],
        out_specs=pl.BlockSpec((tq, w), lambda b, h, i: (b * nq + i, h)),
        scratch_shapes=[pltpu.VMEM((g, SB_DIM, tq), f32)],
        compiler_params=_cparams(("arbitrary", "arbitrary", "arbitrary")),
        name="sb_attn",
    )(proj, proj, proj)


def _merge_out_kernel(ya_ref, yb_ref, ga_ref, gb_ref, x_ref, ada_ref, wa_ref, wb_ref, wo_ref, x1_ref):
    sig = lambda ref: 1.0 / (1.0 + jnp.exp(-ref[...].astype(f32)))
    merged = sig(ga_ref) * _dot(ya_ref[...], wa_ref[...]) + sig(gb_ref) * _dot(yb_ref[...], wb_ref[...])
    x1_ref[...] = x_ref[...] + ada_ref[0, 2:3, :] * _dot(merged.astype(bf16), wo_ref[...])


def _merge_out(ya, yb, proj, x2, ada3, wa, wb, wo, seq):
    t, d = x2.shape
    tm = 256
    per_b = seq // tm
    const = lambda shape: pl.BlockSpec(shape, lambda i: (0, 0), pipeline_mode=pl.Buffered(1))
    return pl.pallas_call(
        _merge_out_kernel,
        out_shape=jax.ShapeDtypeStruct((t, d), f32),
        grid=(t // tm,),
        in_specs=[pl.BlockSpec((tm, HEADS_W), lambda i: (i, 0)),
                  pl.BlockSpec((tm, HEADS_W), lambda i: (i, 0)),
                  pl.BlockSpec((tm, d), lambda i: (i, COL_GA // d)),
                  pl.BlockSpec((tm, d), lambda i: (i, COL_GB // d)),
                  pl.BlockSpec((tm, d), lambda i: (i, 0)),
                  pl.BlockSpec((1, 6, d), lambda i: (i // per_b, 0, 0)),
                  const((HEADS_W, d)), const((HEADS_W, d)), const((d, d))],
        out_specs=pl.BlockSpec((tm, d), lambda i: (i, 0)),
        compiler_params=_cparams(("arbitrary",)),
        name="merge_out",
    )(ya, yb, proj, proj, x2, ada3, wa, wb, wo)


def _ffn_kernel(x1_ref, ada_ref, g2_ref, wg_ref, wu_ref, wo_ref, o_ref, h_ref):
    f = pl.program_id(1)

    @pl.when(f == 0)
    def _():
        x1 = x1_ref[...]
        d = x1.shape[1]
        inv = lax.rsqrt(_row_sumsq(x1) * (1.0 / d) + EPS)
        inv = jnp.concatenate([inv] * (d // LANES), axis=1)
        h = x1 * inv * (g2_ref[...] * (1.0 + ada_ref[0, 4:5, :])) + ada_ref[0, 3:4, :]
        h_ref[...] = h.astype(bf16)
        o_ref[...] = jnp.zeros_like(o_ref)

    h = h_ref[...]
    gate = _dot(h, wg_ref[...])
    up = _dot(h, wu_ref[...])
    act = gate * (1.0 / (1.0 + jnp.exp(-gate))) * up
    o_ref[...] += _dot(act.astype(bf16), wo_ref[...])

    @pl.when(f == pl.num_programs(1) - 1)
    def _():
        o_ref[...] = x1_ref[...] + ada_ref[0, 5:6, :] * o_ref[...]


def _ffn(x1, ada3, g2, w_in, wo, seq):
    t, d = x1.shape
    d_ff = wo.shape[0]
    tm, tf = 1024, 512
    per_b = seq // tm
    n_f = d_ff // tf
    return pl.pallas_call(
        _ffn_kernel,
        out_shape=jax.ShapeDtypeStruct((t, d), f32),
        grid=(t // tm, n_f),
        in_specs=[pl.BlockSpec((tm, d), lambda i, f: (i, 0)),
                  pl.BlockSpec((1, 6, d), lambda i, f: (i // per_b, 0, 0)),
                  pl.BlockSpec((1, d), lambda i, f: (0, 0)),
                  pl.BlockSpec((d, tf), lambda i, f: (0, f)),
                  pl.BlockSpec((d, tf), lambda i, f: (0, n_f + f)),
                  pl.BlockSpec((tf, d), lambda i, f: (f, 0))],
        out_specs=pl.BlockSpec((tm, d), lambda i, f: (i, 0)),
        scratch_shapes=[pltpu.VMEM((tm, d), bf16)],
        compiler_params=_cparams(("arbitrary", "arbitrary")),
        name="ffn",
    )(x1, ada3, g2, w_in, w_in, wo)


def _score_shift(g_q_head, g_k_head):
    c = LOG2_E * QK_DIM ** -0.5
    return 1.02 * QK_DIM * c * jnp.max(jnp.abs(g_q_head)) * jnp.max(jnp.abs(g_k_head))


def _with_swapped(w, axis):
    x1, x2 = jnp.split(w, 2, axis=axis)
    shape = list(w.shape)
    shape[axis] = LANES - ROPE
    z = jnp.zeros(shape, w.dtype)
    return jnp.concatenate([x1, x2, z, x2, x1, z], axis=axis)


def _prep_layer(w_uq, w_ukv, g_q_head, g_k_head):
    with_swapped = _with_swapped
    wq = w_uq.reshape(LATENT, N_HEADS, QK_DIM).transpose(1, 0, 2)
    wq_h = jnp.concatenate([wq[..., :NOPE], with_swapped(wq[..., NOPE:], 2)], axis=2).astype(bf16)
    wkv_h = w_ukv.reshape(LATENT, N_HEADS, NOPE + V_DIM).transpose(1, 0, 2).astype(bf16)

    def head_gain(g):
        g1, g2 = g[NOPE:NOPE + HALF], g[NOPE + HALF:]
        z = jnp.zeros((LANES - ROPE,), g.dtype)
        return jnp.concatenate([g[:NOPE], g1, g2, z, -g2, g1, z])[None, :]

    return dict(wq_h=wq_h, wkv_h=wkv_h, gqh=head_gain(g_q_head), gkh=head_gain(g_k_head))


def kernel(x, c, positions, w_ada, b_ada, g_norm1, g_norm2, w_in, g_q_latent, g_kv_latent, w_uq, w_ukv,
           g_q_head, g_k_head, w_proj_mla, w_proj_sb, w_out, w_ffn_in, w_ffn_out):
    batch, seq, d = x.shape
    depth = w_ada.shape[0]
    t = batch * seq
    x2 = x.reshape(t, d)
    pos2 = positions.reshape(t, 1)
    c_pad = jnp.pad(c, ((0, 8 - batch), (0, 0)))
    lane = np.arange(LANES)
    freq = jnp.where(lane < ROPE, ROPE_THETA ** (-jnp.asarray(lane % HALF, f32) / HALF), 0.0)[None, :]

    for l in range(depth):
        p = _prep_layer(w_uq[l], w_ukv[l], g_q_head[l], g_k_head[l])
        w_in_t = w_in[l].T
        ada, w_main = _ada(c_pad, w_ada[l], b_ada[l][None, :], w_in_t)
        ada3 = ada[:batch].reshape(batch, 6, d)
        w_kpe = _with_swapped(w_in_t[2 * LATENT:2 * LATENT + ROPE], 0).astype(bf16)
        side = (w_proj_mla[l], w_proj_sb[l], w_out[l], w_ffn_in[l], w_ffn_out[l])
        proj, kpe, (wa, wb, wo, w_ffn, wf) = _in_proj(x2, ada3, g_norm1[l][None, :], w_main, w_kpe, side, seq)
        shift = _score_shift(g_q_head[l], g_k_head[l])
        use_shift = shift <= SHIFT_MAX
        pad_lane = (lane == ROPE).astype(f32)[None, :]
        padq = pad_lane * use_shift.astype(f32)
        padk = pad_lane * jnp.where(use_shift, -shift, 0.0)
        q, k, vt = _latent_up(proj, kpe, pos2, freq, g_q_latent[l][None, :], g_kv_latent[l][None, :],
                              p["wq_h"], p["wkv_h"], p["gqh"], p["gkh"], padq, padk, batch, seq)
        ya = lax.cond(use_shift, functools.partial(_mla_attn, shifted=True),
                      functools.partial(_mla_attn, shifted=False), q, k, vt)
        yb = _sb_attn(proj, batch, seq)
        x1 = _merge_out(ya, yb, proj, x2, ada3, wa, wb, wo, seq)
        x2 = _ffn(x1, ada3, g_norm2[l][None, :], w_ffn, wf, seq)
    return x2.reshape(batch, seq, d)
```
